```python
import math
import jax, jax.numpy as jnp
from jax import lax
import numpy as np

D_MODEL = 1024
BATCH = 8
SEQ = 4096
DEPTH = 4

N_A_LAYERS = DEPTH // 2
N_B_LAYERS = DEPTH - N_A_LAYERS
N_DENSE_LAYERS = (DEPTH + 1) // 2
N_MOE_LAYERS = DEPTH // 2
SB_HEADS = 16
SB_HEAD_DIM = D_MODEL // SB_HEADS
MLA_HEADS = 16
Q_LORA_RANK = 384
KV_LORA_RANK = 256
QK_NOPE_DIM = 64
QK_ROPE_DIM = 32
V_HEAD_DIM = 64
ROPE_THETA = 10000.0
D_FF_DENSE = 2816
N_EXPERTS = 8
TOP_K = 2
D_FF_EXPERT = 3584
Q_BLOCK = 128
EXPERT_BLOCK = 128
EPS = 1e-6

kernel_name = "yoco_stickbreak_mla_moe_trunk"


def rmsnorm(x, g):
    xf = x.astype(jnp.float32)
    y = xf * lax.rsqrt(jnp.mean(xf * xf, axis=-1, keepdims=True) + EPS)
    return (y * g.astype(jnp.float32)).astype(x.dtype)


def rope_tables(seq_len):
    pos = jnp.arange(seq_len, dtype=jnp.float32)
    inv_freq = ROPE_THETA ** (-jnp.arange(0, QK_ROPE_DIM, 2, dtype=jnp.float32) / QK_ROPE_DIM)
    ang = pos[:, None] * inv_freq[None, :]
    ang = jnp.concatenate([ang, ang], axis=-1)
    return jnp.cos(ang), jnp.sin(ang)


def apply_rope(x, cos, sin):
    half = x.shape[-1] // 2
    x1, x2 = x[..., :half], x[..., half:]
    rot = jnp.concatenate([-x2, x1], axis=-1)
    return x * cos.astype(x.dtype) + rot * sin.astype(x.dtype)


def to_query_blocks(t):
    b, s, h, d = t.shape
    return t.reshape(b, s // Q_BLOCK, Q_BLOCK, h, d).transpose(1, 0, 3, 2, 4)


def from_query_blocks(o):
    nb, b, h, qb, d = o.shape
    return o.transpose(1, 0, 3, 2, 4).reshape(b, nb * qb, h * d)


def stick_breaking_attention(xn, w_qkv, w_o):
    b, s, _ = xn.shape
    qkv = (xn @ w_qkv).reshape(b, s, 3, SB_HEADS, SB_HEAD_DIM)
    q, k, v = qkv[:, :, 0], qkv[:, :, 1], qkv[:, :, 2]
    k = k.transpose(0, 2, 1, 3)
    v = v.transpose(0, 2, 1, 3)
    scale = 1.0 / math.sqrt(SB_HEAD_DIM)
    kpos = jnp.arange(s)

    def one_block(args):
        qb, blk = args
        z = jnp.einsum('bhqd,bhkd->bhqk', qb, k, preferred_element_type=jnp.float32) * scale
        qpos = blk * Q_BLOCK + jnp.arange(Q_BLOCK)
        mask = kpos[None, :] < qpos[:, None]
        log_1m_beta = jnp.where(mask, jax.nn.log_sigmoid(-z), 0.0)
        later = lax.cumsum(log_1m_beta, axis=3, reverse=True) - log_1m_beta
        att = jnp.where(mask, jnp.exp(jax.nn.log_sigmoid(z) + later), 0.0)
        return jnp.einsum('bhqk,bhkd->bhqd', att.astype(v.dtype), v)

    o = lax.map(one_block, (to_query_blocks(q), jnp.arange(s // Q_BLOCK)))
    return from_query_blocks(o) @ w_o


def mla_shared_kv(h, g_src, w_kv_a, g_kv_a, w_kv_b, g_k_nope, g_k_pe, cos, sin):
    b, s, _ = h.shape
    hn = rmsnorm(h, g_src)
    kv_a = hn @ w_kv_a
    c_kv = rmsnorm(kv_a[..., :KV_LORA_RANK], g_kv_a)
    k_pe = apply_rope(rmsnorm(kv_a[..., KV_LORA_RANK:], g_k_pe), cos, sin)
    kv = (c_kv @ w_kv_b).reshape(b, s, MLA_HEADS, QK_NOPE_DIM + V_HEAD_DIM)
    k_nope = rmsnorm(kv[..., :QK_NOPE_DIM], g_k_nope).transpose(0, 2, 1, 3)
    v = kv[..., QK_NOPE_DIM:].transpose(0, 2, 1, 3)
    return k_nope, k_pe, v


def mla_attention(xn, w_q_a, g_q_a, w_q_b, g_q_nope, g_q_pe, w_o, k_nope, k_pe, v, cos, sin):
    b, s, _ = xn.shape
    q = (rmsnorm(xn @ w_q_a, g_q_a) @ w_q_b).reshape(b, s, MLA_HEADS, QK_NOPE_DIM + QK_ROPE_DIM)
    q_nope = rmsnorm(q[..., :QK_NOPE_DIM], g_q_nope)
    q_pe = apply_rope(rmsnorm(q[..., QK_NOPE_DIM:], g_q_pe), cos[:, None, :], sin[:, None, :])
    scale = 1.0 / math.sqrt(QK_NOPE_DIM + QK_ROPE_DIM)
    kpos = jnp.arange(s)

    def one_block(args):
        qn, qp, blk = args
        sc = (jnp.einsum('bhqd,bhkd->bhqk', qn, k_nope, preferred_element_type=jnp.float32)
              + jnp.einsum('bhqr,bkr->bhqk', qp, k_pe, preferred_element_type=jnp.float32)) * scale
        qpos = blk * Q_BLOCK + jnp.arange(Q_BLOCK)
        mask = kpos[None, :] <= qpos[:, None]
        p = jax.nn.softmax(jnp.where(mask, sc, -jnp.inf), axis=-1)
        return jnp.einsum('bhqk,bhkd->bhqd', p.astype(v.dtype), v)

    o = lax.map(one_block, (to_query_blocks(q_nope), to_query_blocks(q_pe), jnp.arange(s // Q_BLOCK)))
    return from_query_blocks(o) @ w_o


def swiglu(x, w_gate, w_up, w_down):
    return (jax.nn.silu(x @ w_gate) * (x @ w_up)) @ w_down


def moe_swiglu(x2d, w_router, b_router, w_gate, w_up, w_down):
    t, d = x2d.shape
    logits = (x2d @ w_router).astype(jnp.float32) + b_router.astype(jnp.float32)
    top_vals, top_idx = lax.top_k(logits, TOP_K)
    gates = jax.nn.softmax(top_vals, axis=-1).astype(x2d.dtype)
    n_assign = t * TOP_K
    flat_e = top_idx.reshape(-1)
    flat_tok = jnp.repeat(jnp.arange(t), TOP_K)
    flat_g = gates.reshape(-1)
    order = jnp.argsort(flat_e)
    sorted_e, sorted_tok, sorted_g = flat_e[order], flat_tok[order], flat_g[order]
    counts = jnp.zeros((N_EXPERTS,), jnp.int32).at[flat_e].add(1)
    starts = jnp.cumsum(counts) - counts
    padded = ((counts + EXPERT_BLOCK - 1) // EXPERT_BLOCK) * EXPERT_BLOCK
    pends = jnp.cumsum(padded)
    pstarts = pends - padded
    dest = pstarts[sorted_e] + (jnp.arange(n_assign) - starts[sorted_e])
    n_rows = n_assign + N_EXPERTS * EXPERT_BLOCK
    n_rows = ((n_rows + EXPERT_BLOCK - 1) // EXPERT_BLOCK) * EXPERT_BLOCK
    n_blocks = n_rows // EXPERT_BLOCK
    buf = jnp.zeros((n_rows, d), x2d.dtype).at[dest].set(x2d[sorted_tok])
    block_e = jnp.clip(jnp.searchsorted(pends, jnp.arange(n_blocks) * EXPERT_BLOCK, side='right'),
                       0, N_EXPERTS - 1)

    def expert_block(args):
        xb, e = args
        return (jax.nn.silu(xb @ w_gate[e]) * (xb @ w_up[e])) @ w_down[e]

    out = lax.map(expert_block, (buf.reshape(n_blocks, EXPERT_BLOCK, d), block_e)).reshape(n_rows, d)
    rows = out[dest] * sorted_g[:, None]
    return jax.ops.segment_sum(rows, sorted_tok, num_segments=t)


def setup_inputs(seed: int = 0) -> dict:
    key = jax.random.key(seed)
    ks = iter(jax.random.split(key, 32))

    def w(shape, fan_in):
        return jax.random.normal(next(ks), shape, jnp.float32) * (fan_in ** -0.5)

    def g(shape):
        return 1.0 + 0.02 * jax.random.normal(next(ks), shape, jnp.float32)

    sb_w = SB_HEADS * SB_HEAD_DIM
    return {
        "x": jax.random.normal(next(ks), (BATCH, SEQ, D_MODEL), jnp.float32),
        "g_mix": g((DEPTH, D_MODEL)),
        "g_ffn": g((DEPTH, D_MODEL)),
        "sb_w_qkv": w((N_A_LAYERS, D_MODEL, 3 * sb_w), D_MODEL),
        "sb_w_o": w((N_A_LAYERS, sb_w, D_MODEL), sb_w),
        "kv_g_src": g((D_MODEL,)),
        "kv_w_a": w((D_MODEL, KV_LORA_RANK + QK_ROPE_DIM), D_MODEL),
        "kv_g_a": g((KV_LORA_RANK,)),
        "kv_w_b": w((KV_LORA_RANK, MLA_HEADS * (QK_NOPE_DIM + V_HEAD_DIM)), KV_LORA_RANK),
        "kv_g_k_nope": g((QK_NOPE_DIM,)),
        "kv_g_k_pe": g((QK_ROPE_DIM,)),
        "mla_w_q_a": w((N_B_LAYERS, D_MODEL, Q_LORA_RANK), D_MODEL),
        "mla_g_q_a": g((N_B_LAYERS, Q_LORA_RANK)),
        "mla_w_q_b": w((N_B_LAYERS, Q_LORA_RANK, MLA_HEADS * (QK_NOPE_DIM + QK_ROPE_DIM)), Q_LORA_RANK),
        "mla_g_q_nope": g((N_B_LAYERS, QK_NOPE_DIM)),
        "mla_g_q_pe": g((N_B_LAYERS, QK_ROPE_DIM)),
        "mla_w_o": w((N_B_LAYERS, MLA_HEADS * V_HEAD_DIM, D_MODEL), MLA_HEADS * V_HEAD_DIM),
        "ffn_w_gate": w((N_DENSE_LAYERS, D_MODEL, D_FF_DENSE), D_MODEL),
        "ffn_w_up": w((N_DENSE_LAYERS, D_MODEL, D_FF_DENSE), D_MODEL),
        "ffn_w_down": w((N_DENSE_LAYERS, D_FF_DENSE, D_MODEL), D_FF_DENSE),
        "moe_w_router": w((N_MOE_LAYERS, D_MODEL, N_EXPERTS), D_MODEL),
        "moe_b_router": 0.01 * jax.random.normal(next(ks), (N_MOE_LAYERS, N_EXPERTS), jnp.float32),
        "moe_w_gate": w((N_MOE_LAYERS, N_EXPERTS, D_MODEL, D_FF_EXPERT), D_MODEL),
        "moe_w_up": w((N_MOE_LAYERS, N_EXPERTS, D_MODEL, D_FF_EXPERT), D_MODEL),
        "moe_w_down": w((N_MOE_LAYERS, N_EXPERTS, D_FF_EXPERT, D_MODEL), D_FF_EXPERT),
    }


def reference(x, g_mix, g_ffn, sb_w_qkv, sb_w_o,
              kv_g_src, kv_w_a, kv_g_a, kv_w_b, kv_g_k_nope, kv_g_k_pe,
              mla_w_q_a, mla_g_q_a, mla_w_q_b, mla_g_q_nope, mla_g_q_pe, mla_w_o,
              ffn_w_gate, ffn_w_up, ffn_w_down,
              moe_w_router, moe_b_router, moe_w_gate, moe_w_up, moe_w_down):
    b, s, d = x.shape
    cos, sin = rope_tables(s)
    k_nope = k_pe = v = None
    for i in range(DEPTH):
        if i < N_A_LAYERS:
            x = x + stick_breaking_attention(rmsnorm(x, g_mix[i]), sb_w_qkv[i], sb_w_o[i])
        else:
            if i == N_A_LAYERS:
                k_nope, k_pe, v = mla_shared_kv(x, kv_g_src, kv_w_a, kv_g_a, kv_w_b,
                                                kv_g_k_nope, kv_g_k_pe, cos, sin)
            j = i - N_A_LAYERS
            x = x + mla_attention(rmsnorm(x, g_mix[i]), mla_w_q_a[j], mla_g_q_a[j], mla_w_q_b[j],
                                  mla_g_q_nope[j], mla_g_q_pe[j], mla_w_o[j],
                                  k_nope, k_pe, v, cos, sin)
        hn = rmsnorm(x, g_ffn[i])
        if i % 2 == 0:
            x = x + swiglu(hn, ffn_w_gate[i // 2], ffn_w_up[i // 2], ffn_w_down[i // 2])
        else:
            m = i // 2
            x = x + moe_swiglu(hn.reshape(b * s, d), moe_w_router[m], moe_b_router[m],
                               moe_w_gate[m], moe_w_up[m], moe_w_down[m]).reshape(b, s, d)
    return x
```

```python
import functools
import math

import jax
import jax.numpy as jnp
from jax import lax
from jax.experimental import pallas as pl
from jax.experimental.pallas import tpu as pltpu

F32 = jnp.float32
BF16 = jnp.bfloat16
I32 = jnp.int32

EPS = 1e-6
N_HEADS = 16
SB_HEAD_DIM = 64
NOPE_DIM = 64
ROPE_DIM = 32
V_DIM = 64
KV_RANK = 256
ROPE_THETA = 10000.0
N_EXPERTS = 8
LANES = 128
PAIR_W = 2 * NOPE_DIM + LANES
LOG2E = 1.4426950408889634
SB_SKIP_EXP = 110.0
VMEM_LIMIT = 52 * 1024 * 1024


def _params(sem, vmem=VMEM_LIMIT):
    return pltpu.CompilerParams(dimension_semantics=sem, vmem_limit_bytes=vmem)


def _dot(a, b):
    return jnp.dot(a, b, preferred_element_type=F32)


def _dot_nt(a, b):
    return lax.dot_general(a, b, (((1,), (1,)), ((), ())), preferred_element_type=F32)


def _split_bf16(x):
    hi = x.astype(BF16)
    lo = (x - hi.astype(F32)).astype(BF16)
    return hi, lo


def _dot_hilo(x, w):
    hi, lo = _split_bf16(x)
    return _dot(hi, w) + _dot(lo, w)


def _rms_matmul_body(x_ref, g_ref, w_ref, o_ref, xn_ref):
    @pl.when(pl.program_id(1) == 0)
    def _():
        x = x_ref[...]
        inv = lax.rsqrt(jnp.mean(x * x, axis=-1, keepdims=True) + EPS)
        xn_ref[...] = (x * inv * g_ref[...]).astype(BF16)

    o_ref[...] = _dot(xn_ref[...], w_ref[...]).astype(o_ref.dtype)


def rms_matmul(x, g, w, out_dtype, tm, tn):
    t, d = x.shape
    n = w.shape[1]
    return pl.pallas_call(
        _rms_matmul_body,
        grid=(t // tm, n // tn),
        in_specs=[pl.BlockSpec((tm, d), lambda i, j: (i, 0)),
                  pl.BlockSpec((1, d), lambda i, j: (0, 0)),
                  pl.BlockSpec((d, tn), lambda i, j: (0, j))],
        out_specs=pl.BlockSpec((tm, tn), lambda i, j: (i, j)),
        out_shape=jax.ShapeDtypeStruct((t, n), out_dtype),
        scratch_shapes=[pltpu.VMEM((tm, d), BF16)],
        compiler_params=_params(("parallel", "arbitrary")),
        name="rms_matmul",
    )(x, g.reshape(1, d), w)


def _matmul_res_body(a_ref, w_ref, r_ref, o_ref):
    o_ref[...] = r_ref[...] + _dot(a_ref[...], w_ref[...])


def matmul_residual(a, w, res, tm):
    t, k = a.shape
    n = w.shape[1]
    return pl.pallas_call(
        _matmul_res_body,
        grid=(t // tm,),
        in_specs=[pl.BlockSpec((tm, k), lambda i: (i, 0)),
                  pl.BlockSpec((k, n), lambda i: (0, 0)),
                  pl.BlockSpec((tm, n), lambda i: (i, 0))],
        out_specs=pl.BlockSpec((tm, n), lambda i: (i, 0)),
        out_shape=jax.ShapeDtypeStruct((t, n), F32),
        compiler_params=_params(("parallel",)),
        name="matmul_residual",
    )(a, w, res)


def _softplus(z):
    return jnp.maximum(z, 0.0) + jnp.log(1.0 + jnp.exp(-jnp.abs(z)))


def _sb_attn_body(q_ref, k_ref, v_ref, tri_ref, o_ref, kmax_ref, *, tq):
    qi = pl.program_id(2)
    lane = lax.broadcasted_iota(I32, (1, LANES), 1)

    @pl.when(qi == 0)
    def _():
        k = k_ref[0].astype(F32)
        k2 = k * k
        for hh in range(2):
            n2 = jnp.sum(jnp.where((lane // SB_HEAD_DIM) == hh, k2, 0.0), axis=-1, keepdims=True)
            kmax_ref[hh] = jnp.max(n2, axis=0, keepdims=True)

    tri = tri_ref[...]
    q2 = q_ref[0]
    row = lax.broadcasted_iota(I32, (tq, tq), 0)
    col = lax.broadcasted_iota(I32, (tq, tq), 1)
    causal = col < row
    outs = []
    for hh in range(2):
        qm = jnp.where((lane // SB_HEAD_DIM) == hh, q2, jnp.zeros_like(q2))
        qf = qm.astype(F32)
        qnorm = jnp.sqrt(jnp.sum(qf * qf, axis=-1, keepdims=True))
        zbound = qnorm * jnp.sqrt(kmax_ref[hh]) * 1.001 + 1e-3

        start = pl.multiple_of(qi * tq, tq)
        kb = k_ref[0, pl.ds(start, tq), :]
        vb = v_ref[0, pl.ds(start, tq), :]
        z = _dot_nt(qm, kb)
        l1m = jnp.where(causal, -_softplus(z), 0.0)
        csum = _dot_hilo(l1m, tri)
        att = jnp.where(causal, jnp.exp(z + csum), 0.0)
        acc = _dot(att.astype(BF16), vb)
        carry = jnp.sum(l1m, axis=-1, keepdims=True)

        def cond(state):
            jj, carry, _ = state
            live = jnp.max(carry + zbound) > -SB_SKIP_EXP
            return jnp.logical_and(jj <= qi, live)

        def body(state):
            jj, carry, acc = state
            start = pl.multiple_of((qi - jj) * tq, tq)
            kb = k_ref[0, pl.ds(start, tq), :]
            vb = v_ref[0, pl.ds(start, tq), :]
            z = _dot_nt(qm, kb)
            l1m = -_softplus(z)
            csum = carry + _dot_hilo(l1m, tri)
            att = jnp.exp(z + csum)
            acc = acc + _dot(att.astype(BF16), vb)
            carry = carry + jnp.sum(l1m, axis=-1, keepdims=True)
            return jj + 1, carry, acc

        _, _, acc = lax.while_loop(cond, body, (jnp.int32(1), carry, acc))
        outs.append(acc)

    o_ref[0] = jnp.where(lane < SB_HEAD_DIM, outs[0], outs[1]).astype(o_ref.dtype)


def sb_attention(qkv, tq):
    b, s, _ = qkv.shape
    n_pairs = N_HEADS // 2
    ji = lax.broadcasted_iota(I32, (tq, tq), 0)
    si = lax.broadcasted_iota(I32, (tq, tq), 1)
    tri = (ji >= si).astype(BF16)
    return pl.pallas_call(
        functools.partial(_sb_attn_body, tq=tq),
        grid=(b, n_pairs, s // tq),
        in_specs=[pl.BlockSpec((1, tq, LANES), lambda bi, p, i: (bi, i, p)),
                  pl.BlockSpec((1, s, LANES), lambda bi, p, i: (bi, 0, n_pairs + p)),
                  pl.BlockSpec((1, s, LANES), lambda bi, p, i: (bi, 0, 2 * n_pairs + p)),
                  pl.BlockSpec((tq, tq), lambda bi, p, i: (0, 0))],
        out_specs=pl.BlockSpec((1, tq, LANES), lambda bi, p, i: (bi, i, p)),
        out_shape=jax.ShapeDtypeStruct((b, s, N_HEADS * SB_HEAD_DIM), BF16),
        scratch_shapes=[pltpu.VMEM((2, 1, 1), F32)],
        compiler_params=_params(("parallel", "parallel", "arbitrary")),
        name="sb_attention",
    )(qkv, qkv, qkv, tri)


def _mla_attn_body(q_ref, k_ref, v_ref, o_ref, *, tq):
    qi = pl.program_id(2)
    lane_q = lax.broadcasted_iota(I32, (1, PAIR_W), 1)
    lane_o = lax.broadcasted_iota(I32, (1, LANES), 1)
    q2 = q_ref[0]
    row = lax.broadcasted_iota(I32, (tq, tq), 0)
    col = lax.broadcasted_iota(I32, (tq, tq), 1)
    causal = col <= row
    outs = []
    for hh in range(2):
        nope = (lane_q // NOPE_DIM) == hh
        pe = ((lane_q - 2 * NOPE_DIM) // ROPE_DIM) == hh
        qm = jnp.where(jnp.logical_or(nope, jnp.logical_and(lane_q >= 2 * NOPE_DIM, pe)),
                       q2, jnp.zeros_like(q2))

        start = pl.multiple_of(qi * tq, tq)
        sc = _dot_nt(qm, k_ref[0, pl.ds(start, tq), :])
        sc = jnp.where(causal, sc, -jnp.inf)
        m = jnp.max(sc, axis=-1, keepdims=True)
        p = jnp.exp2(sc - m)
        l = jnp.sum(p, axis=-1, keepdims=True)
        acc = _dot(p.astype(BF16), v_ref[0, pl.ds(start, tq), :])

        def body(j, state):
            m, l, acc = state
            start = pl.multiple_of(j * tq, tq)
            sc = _dot_nt(qm, k_ref[0, pl.ds(start, tq), :])
            m_new = jnp.maximum(m, jnp.max(sc, axis=-1, keepdims=True))
            alpha = jnp.exp2(m - m_new)
            p = jnp.exp2(sc - m_new)
            l = alpha * l + jnp.sum(p, axis=-1, keepdims=True)
            acc = alpha * acc + _dot(p.astype(BF16), v_ref[0, pl.ds(start, tq), :])
            return m_new, l, acc

        m, l, acc = lax.fori_loop(0, qi, body, (m, l, acc))
        outs.append(acc / l)

    o_ref[0] = jnp.where(lane_o < V_DIM, outs[0], outs[1]).astype(o_ref.dtype)


def mla_attention(qcat, kcat, v, tq):
    b, s, _ = qcat.shape
    n_pairs = N_HEADS // 2
    return pl.pallas_call(
        functools.partial(_mla_attn_body, tq=tq),
        grid=(b, n_pairs, s // tq),
        in_specs=[pl.BlockSpec((1, tq, PAIR_W), lambda bi, p, i: (bi, i, p)),
                  pl.BlockSpec((1, s, PAIR_W), lambda bi, p, i: (bi, 0, p)),
                  pl.BlockSpec((1, s, LANES), lambda bi, p, i: (bi, 0, p))],
        out_specs=pl.BlockSpec((1, tq, LANES), lambda bi, p, i: (bi, i, p)),
        out_shape=jax.ShapeDtypeStruct((b, s, N_HEADS * V_DIM), BF16),
        compiler_params=_params(("parallel", "parallel", "arbitrary")),
        name="mla_attention",
    )(qcat, kcat, v)


def _group_rms(x, gmat, group, g):
    chunks = []
    for c in range(x.shape[1] // 256):
        xc = x[:, c * 256:(c + 1) * 256]
        ss = _dot_hilo(xc * xc, gmat)
        chunks.append(xc * lax.rsqrt(ss * (1.0 / group) + EPS))
    y = chunks[0] if len(chunks) == 1 else jnp.concatenate(chunks, axis=-1)
    return y * g


def _rope128(x, cos, sin):
    lane = lax.broadcasted_iota(I32, (1, LANES), 1)
    first_half = (lane % ROPE_DIM) < (ROPE_DIM // 2)
    rot = jnp.where(first_half,
                    -pltpu.roll(x, LANES - ROPE_DIM // 2, 1),
                    pltpu.roll(x, ROPE_DIM // 2, 1))
    return x * cos + rot * sin


def _kv_prep_body(kva_ref, g_a_ref, g_pe_ref, cos_ref, sin_ref, wn_ref, wv_ref, g_n_ref,
                  g64_ref, k_ref, v_ref):
    lane = lax.broadcasted_iota(I32, (1, LANES), 1)
    kva = kva_ref[...]
    c = kva[:, :KV_RANK]
    cn = (c * lax.rsqrt(jnp.mean(c * c, axis=-1, keepdims=True) + EPS) * g_a_ref[...]).astype(BF16)
    pe = kva[:, KV_RANK:KV_RANK + LANES]
    ms = jnp.sum(pe * pe, axis=-1, keepdims=True) * (1.0 / ROPE_DIM)
    pen = pe * lax.rsqrt(ms + EPS) * g_pe_ref[...]
    kpe = _rope128(pen, cos_ref[...], sin_ref[...])
    kpe = jnp.where(lane < ROPE_DIM, kpe, 0.0)
    kpe2 = (kpe + pltpu.roll(kpe, ROPE_DIM, 1)).astype(BF16)
    kn = _group_rms(_dot(cn, wn_ref[...]), g64_ref[...], NOPE_DIM, g_n_ref[...]).astype(BF16)
    v_ref[...] = _dot(cn, wv_ref[...]).astype(BF16)
    for p in range(N_HEADS // 2):
        k_ref[:, p * PAIR_W:p * PAIR_W + LANES] = kn[:, p * LANES:(p + 1) * LANES]
        k_ref[:, p * PAIR_W + LANES:(p + 1) * PAIR_W] = kpe2


def _q_prep_body(qa_ref, g_a_ref, wn_ref, wp_ref, g_n_ref, g_p_ref, cos_ref, sin_ref,
                 g64_ref, g32_ref, q_ref, *, scale):
    lane = lax.broadcasted_iota(I32, (1, LANES), 1)
    qa = qa_ref[...]
    qan = (qa * lax.rsqrt(jnp.mean(qa * qa, axis=-1, keepdims=True) + EPS) * g_a_ref[...]).astype(BF16)
    qn = _group_rms(_dot(qan, wn_ref[...]), g64_ref[...], NOPE_DIM, g_n_ref[...]) * scale
    qp = _group_rms(_dot(qan, wp_ref[...]), g32_ref[...], ROPE_DIM, g_p_ref[...])
    cos = cos_ref[...]
    sin = sin_ref[...]
    for c in range(N_HEADS * ROPE_DIM // LANES):
        pe4 = _rope128(qp[:, c * LANES:(c + 1) * LANES], cos, sin) * scale
        for half in range(2):
            p = 2 * c + half
            x = pe4 if half == 0 else pltpu.roll(pe4, 2 * ROPE_DIM, 1)
            q_ref[:, p * PAIR_W:p * PAIR_W + LANES] = qn[:, p * LANES:(p + 1) * LANES].astype(BF16)
            q_ref[:, p * PAIR_W + LANES:(p + 1) * PAIR_W] = jnp.where(lane < 2 * ROPE_DIM, x, 0.0).astype(BF16)


def _block_diag_ones(n, group):
    a = lax.broadcasted_iota(I32, (n, n), 0) // group
    b = lax.broadcasted_iota(I32, (n, n), 1) // group
    return (a == b).astype(BF16)


def _full(shape):
    return pl.BlockSpec(shape, lambda i: (0,) * len(shape))


def kv_prep(kva, g_a, g_pe_pad, cos_t, sin_t, wn, wv, g_n_t, tm, seq):
    t = kva.shape[0]
    nseq = seq // tm
    return pl.pallas_call(
        _kv_prep_body,
        grid=(t // tm,),
        in_specs=[pl.BlockSpec((tm, kva.shape[1]), lambda i: (i, 0)),
                  _full((1, KV_RANK)), _full((1, LANES)),
                  pl.BlockSpec((tm, LANES), lambda i: (i % nseq, 0)),
                  pl.BlockSpec((tm, LANES), lambda i: (i % nseq, 0)),
                  _full(wn.shape), _full(wv.shape), _full((1, wn.shape[1])),
                  _full((256, 256))],
        out_specs=[pl.BlockSpec((tm, (N_HEADS // 2) * PAIR_W), lambda i: (i, 0)),
                   pl.BlockSpec((tm, N_HEADS * V_DIM), lambda i: (i, 0))],
        out_shape=[jax.ShapeDtypeStruct((t, (N_HEADS // 2) * PAIR_W), BF16),
                   jax.ShapeDtypeStruct((t, N_HEADS * V_DIM), BF16)],
        compiler_params=_params(("parallel",)),
        name="kv_prep",
    )(kva, g_a, g_pe_pad, cos_t, sin_t, wn, wv, g_n_t, _block_diag_ones(256, NOPE_DIM))


def q_prep(qa, g_a, wn, wp, g_n_t, g_p_t, cos_t, sin_t, tm, seq, scale):
    t = qa.shape[0]
    nseq = seq // tm
    return pl.pallas_call(
        functools.partial(_q_prep_body, scale=scale),
        grid=(t // tm,),
        in_specs=[pl.BlockSpec((tm, qa.shape[1]), lambda i: (i, 0)),
                  _full((1, qa.shape[1])), _full(wn.shape), _full(wp.shape),
                  _full((1, wn.shape[1])), _full((1, wp.shape[1])),
                  pl.BlockSpec((tm, LANES), lambda i: (i % nseq, 0)),
                  pl.BlockSpec((tm, LANES), lambda i: (i % nseq, 0)),
                  _full((256, 256)), _full((256, 256))],
        out_specs=pl.BlockSpec((tm, (N_HEADS // 2) * PAIR_W), lambda i: (i, 0)),
        out_shape=jax.ShapeDtypeStruct((t, (N_HEADS // 2) * PAIR_W), BF16),
        compiler_params=_params(("parallel",)),
        name="q_prep",
    )(qa, g_a, wn, wp, g_n_t, g_p_t, cos_t, sin_t,
      _block_diag_ones(256, NOPE_DIM), _block_diag_ones(256, ROPE_DIM))


def _ffn_body(x_ref, g_ref, wg_ref, wu_ref, wd_ref, o_ref, xn_ref, acc_ref):
    j = pl.program_id(1)

    @pl.when(j == 0)
    def _():
        x = x_ref[...]
        inv = lax.rsqrt(jnp.mean(x * x, axis=-1, keepdims=True) + EPS)
        xn_ref[...] = (x * inv * g_ref[...]).astype(BF16)

    xn = xn_ref[...]
    gate = _dot(xn, wg_ref[...])
    up = _dot(xn, wu_ref[...])
    h = (gate * jax.nn.sigmoid(gate) * up).astype(BF16)
    contrib = _dot(h, wd_ref[...])

    @pl.when(j == 0)
    def _():
        acc_ref[...] = contrib

    @pl.when(j > 0)
    def _():
        acc_ref[...] += contrib

    @pl.when(j == pl.num_programs(1) - 1)
    def _():
        o_ref[...] = x_ref[...] + acc_ref[...]


def dense_ffn(x, g, wg, wu, wd, tm, tf):
    t, d = x.shape
    f = wg.shape[1]
    return pl.pallas_call(
        _ffn_body,
        grid=(t // tm, f // tf),
        in_specs=[pl.BlockSpec((tm, d), lambda i, j: (i, 0)),
                  pl.BlockSpec((1, d), lambda i, j: (0, 0)),
                  pl.BlockSpec((d, tf), lambda i, j: (0, j)),
                  pl.BlockSpec((d, tf), lambda i, j: (0, j)),
                  pl.BlockSpec((tf, d), lambda i, j: (j, 0))],
        out_specs=pl.BlockSpec((tm, d), lambda i, j: (i, 0)),
        out_shape=jax.ShapeDtypeStruct((t, d), F32),
        scratch_shapes=[pltpu.VMEM((tm, d), BF16), pltpu.VMEM((tm, d), F32)],
        compiler_params=_params(("parallel", "arbitrary")),
        name="dense_ffn",
    )(x, g.reshape(1, d), wg, wu, wd)


def _router_body(x_ref, g_ref, wr_ref, b_ref, upper_ref,
                 xn_ref, idx_ref, gate_ref, rank_ref, cnt_ref, base_ref):
    i = pl.program_id(0)

    @pl.when(i == 0)
    def _():
        base_ref[...] = jnp.zeros_like(base_ref)

    x = x_ref[...]
    xn = x * lax.rsqrt(jnp.mean(x * x, axis=-1, keepdims=True) + EPS) * g_ref[...]
    xn_ref[...] = xn
    tm = x.shape[0]

    x_hi, x_lo = _split_bf16(xn)
    w_hi, w_lo = _split_bf16(wr_ref[...])
    logits = _dot_nt(w_hi, x_hi) + _dot_nt(w_hi, x_lo) + _dot_nt(w_lo, x_hi) + b_ref[...]

    e_iota = lax.broadcasted_iota(I32, (N_EXPERTS, tm), 0)
    m1 = jnp.max(logits, axis=0, keepdims=True)
    i1 = jnp.min(jnp.where(logits == m1, e_iota, N_EXPERTS), axis=0, keepdims=True)
    sel1 = e_iota == i1
    rest = jnp.where(sel1, -jnp.inf, logits)
    m2 = jnp.max(rest, axis=0, keepdims=True)
    i2 = jnp.min(jnp.where(rest == m2, e_iota, N_EXPERTS), axis=0, keepdims=True)
    sel2 = e_iota == i2
    e2 = jnp.exp(m2 - m1)
    g1 = 1.0 / (1.0 + e2)
    idx_ref[...] = jnp.concatenate([i1, i2], axis=0)
    gate_ref[...] = jnp.concatenate([g1, e2 * g1], axis=0)

    member = jnp.logical_or(sel1, sel2)
    prefix = _dot(member.astype(BF16), upper_ref[...])
    rank = prefix + base_ref[...]
    r1 = jnp.sum(jnp.where(sel1, rank, 0.0), axis=0, keepdims=True)
    r2 = jnp.sum(jnp.where(sel2, rank, 0.0), axis=0, keepdims=True)
    rank_ref[...] = jnp.concatenate([r1, r2], axis=0).astype(I32)
    base_ref[...] += jnp.sum(member.astype(F32), axis=1, keepdims=True)
    cnt_ref[...] = jnp.broadcast_to(base_ref[...], cnt_ref.shape).astype(I32)


def moe_router(x, g, w_router, b_router, tm):
    t, d = x.shape
    a = lax.broadcasted_iota(I32, (tm, tm), 0)
    b = lax.broadcasted_iota(I32, (tm, tm), 1)
    upper = (a < b).astype(BF16)
    return pl.pallas_call(
        _router_body,
        grid=(t // tm,),
        in_specs=[pl.BlockSpec((tm, d), lambda i: (i, 0)),
                  _full((1, d)), _full((N_EXPERTS, d)), _full((N_EXPERTS, 1)), _full((tm, tm))],
        out_specs=[pl.BlockSpec((tm, d), lambda i: (i, 0)),
                   pl.BlockSpec((2, tm), lambda i: (0, i)),
                   pl.BlockSpec((2, tm), lambda i: (0, i)),
                   pl.BlockSpec((2, tm), lambda i: (0, i)),
                   _full((N_EXPERTS, LANES))],
        out_shape=[jax.ShapeDtypeStruct((t, d), F32),
                   jax.ShapeDtypeStruct((2, t), I32),
                   jax.ShapeDtypeStruct((2, t), F32),
                   jax.ShapeDtypeStruct((2, t), I32),
                   jax.ShapeDtypeStruct((N_EXPERTS, LANES), I32)],
        scratch_shapes=[pltpu.VMEM((N_EXPERTS, 1), F32)],
        compiler_params=_params(("arbitrary",)),
        name="moe_router",
    )(x, g.reshape(1, d), w_router.T, b_router.reshape(N_EXPERTS, 1), upper)


def _row_copy(src, src_row, dst, dst_row, sem):
    return pltpu.make_async_copy(src.at[pl.ds(src_row, 1)], dst.at[pl.ds(dst_row, 1)], sem)


def _dispatch_body(dest_ref, xn_hbm, buf_in_hbm, buf_hbm, sem, *, tb, t_total):
    del buf_in_hbm
    base = pl.program_id(0) * tb

    def issue(t, c):
        for k in range(2):
            _row_copy(xn_hbm, base + t, buf_hbm, dest_ref[k * t_total + base + t], sem).start()
        return c

    lax.fori_loop(0, tb, issue, 0)

    def drain(t, c):
        for k in range(2):
            _row_copy(xn_hbm, 0, buf_hbm, 0, sem).wait()
        return c

    lax.fori_loop(0, tb, drain, 0)


def moe_dispatch(dest_flat, xn, n_rows, tb):
    t, d = xn.shape
    buf0 = jnp.zeros((n_rows, d), xn.dtype)
    grid_spec = pltpu.PrefetchScalarGridSpec(
        num_scalar_prefetch=1,
        grid=(t // tb,),
        in_specs=[pl.BlockSpec(memory_space=pl.ANY), pl.BlockSpec(memory_space=pl.ANY)],
        out_specs=pl.BlockSpec(memory_space=pl.ANY),
        scratch_shapes=[pltpu.SemaphoreType.DMA(())],
    )
    return pl.pallas_call(
        functools.partial(_dispatch_body, tb=tb, t_total=t),
        grid_spec=grid_spec,
        out_shape=jax.ShapeDtypeStruct((n_rows, d), xn.dtype),
        input_output_aliases={2: 0},
        compiler_params=_params(("arbitrary",)),
        name="moe_dispatch",
    )(dest_flat, xn, buf0)


def _experts_body(be_ref, nv_ref, x_ref, wg_ref, wu_ref, wd_ref, o_ref, xb_ref, acc_ref):
    i = pl.program_id(0)
    j = pl.program_id(1)
    valid = i < nv_ref[0]

    @pl.when(jnp.logical_and(valid, j == 0))
    def _():
        xb_ref[...] = x_ref[...].astype(BF16)

    @pl.when(valid)
    def _():
        xb = xb_ref[...]
        gate = _dot(xb, wg_ref[0])
        up = _dot(xb, wu_ref[0])
        h = (gate * jax.nn.sigmoid(gate) * up).astype(BF16)
        contrib = _dot(h, wd_ref[0])

        @pl.when(j == 0)
        def _():
            acc_ref[...] = contrib

        @pl.when(j > 0)
        def _():
            acc_ref[...] += contrib

    @pl.when(j == pl.num_programs(1) - 1)
    def _():
        @pl.when(valid)
        def _():
            o_ref[...] = acc_ref[...]

        @pl.when(jnp.logical_not(valid))
        def _():
            o_ref[...] = jnp.zeros_like(o_ref)


def moe_experts(block_e, n_valid, buf, wg, wu, wd, tm, tf):
    n_rows, d = buf.shape
    f = wg.shape[2]
    grid_spec = pltpu.PrefetchScalarGridSpec(
        num_scalar_prefetch=2,
        grid=(n_rows // tm, f // tf),
        in_specs=[pl.BlockSpec((tm, d), lambda i, j, be, nv: (i, 0)),
                  pl.BlockSpec((1, d, tf), lambda i, j, be, nv: (be[i], 0, j)),
                  pl.BlockSpec((1, d, tf), lambda i, j, be, nv: (be[i], 0, j)),
                  pl.BlockSpec((1, tf, d), lambda i, j, be, nv: (be[i], j, 0))],
        out_specs=pl.BlockSpec((tm, d), lambda i, j, be, nv: (i, 0)),
        scratch_shapes=[pltpu.VMEM((tm, d), BF16), pltpu.VMEM((tm, d), F32)],
    )
    return pl.pallas_call(
        _experts_body,
        grid_spec=grid_spec,
        out_shape=jax.ShapeDtypeStruct((n_rows, d), F32),
        compiler_params=_params(("parallel", "arbitrary")),
        name="moe_experts",
    )(block_e, n_valid, buf, wg, wu, wd)


def _combine_body(dest_ref, x_ref, gate_ref, eo_hbm, o_ref, buf_ref, sem, *, tb, t_total):
    base = pl.program_id(0) * tb

    def issue(t, c):
        for k in range(2):
            _row_copy(eo_hbm, dest_ref[k * t_total + base + t], buf_ref.at[k], t, sem).start()
        return c

    lax.fori_loop(0, tb, issue, 0)

    def drain(t, c):
        for k in range(2):
            _row_copy(eo_hbm, 0, buf_ref.at[k], 0, sem).wait()
        return c

    lax.fori_loop(0, tb, drain, 0)
    gates = gate_ref[...]
    o_ref[...] = x_ref[...] + gates[:, 0:1] * buf_ref[0] + gates[:, 1:2] * buf_ref[1]


def moe_combine(dest_flat, x, gates_t, expert_out, tb):
    t, d = x.shape
    grid_spec = pltpu.PrefetchScalarGridSpec(
        num_scalar_prefetch=1,
        grid=(t // tb,),
        in_specs=[pl.BlockSpec((tb, d), lambda i, dest: (i, 0)),
                  pl.BlockSpec((tb, 2), lambda i, dest: (i, 0)),
                  pl.BlockSpec(memory_space=pl.ANY)],
        out_specs=pl.BlockSpec((tb, d), lambda i, dest: (i, 0)),
        scratch_shapes=[pltpu.VMEM((2, tb, d), F32), pltpu.SemaphoreType.DMA(())],
    )
    return pl.pallas_call(
        functools.partial(_combine_body, tb=tb, t_total=t),
        grid_spec=grid_spec,
        out_shape=jax.ShapeDtypeStruct((t, d), F32),
        compiler_params=_params(("arbitrary",)),
        name="moe_combine",
    )(dest_flat, x, gates_t, expert_out)


def moe_layer(x, g, w_router, b_router, wg, wu, wd, tiles):
    t, d = x.shape
    tm = tiles["moe_tm"]
    xn, idx, gates, rank, cnt = moe_router(x, g, w_router, b_router, tiles["router_tm"])
    counts = cnt[:, 0]
    padded = ((counts + tm - 1) // tm) * tm
    pends = jnp.cumsum(padded)
    pstarts = pends - padded
    dest_flat = (pstarts[idx] + rank).reshape(-1)
    n_rows = 2 * t + N_EXPERTS * tm
    n_blocks = n_rows // tm
    block_e = jnp.clip(jnp.searchsorted(pends, jnp.arange(n_blocks, dtype=I32) * tm, side="right"),
                       0, N_EXPERTS - 1).astype(I32)
    n_valid = (pends[-1:] // tm).astype(I32)
    buf = moe_dispatch(dest_flat, xn, n_rows, tiles["moe_tb"])
    eo = moe_experts(block_e, n_valid, buf, wg, wu, wd, tm, tiles["moe_tf"])
    return moe_combine(dest_flat, x, gates.T, eo, tiles["moe_tb"])


def _tiles(t, s, d_ff_dense, d_ff_expert):
    def fit(n, want):
        while n % want:
            want //= 2
        return want
    return dict(
        proj_tm=fit(t, 1024), attn_tq=fit(s, 256), prep_tm=fit(s, 512),
        ffn_tm=fit(t, 512), ffn_tf=d_ff_dense // 2 if (d_ff_dense // 2) % LANES == 0 else d_ff_dense,
        router_tm=fit(t, 512), moe_tm=fit(t, 512), moe_tf=fit(d_ff_expert, 512), moe_tb=fit(t, 256),
    )


def _rope_tables(seq):
    pos = jnp.arange(seq, dtype=F32)
    inv_freq = ROPE_THETA ** (-jnp.arange(0, ROPE_DIM, 2, dtype=F32) / ROPE_DIM)
    ang = pos[:, None] * inv_freq[None, :]
    ang = jnp.concatenate([ang, ang], axis=-1)
    reps = LANES // ROPE_DIM
    return jnp.tile(jnp.cos(ang), (1, reps)), jnp.tile(jnp.sin(ang), (1, reps))


def kernel(x, g_mix, g_ffn, sb_w_qkv, sb_w_o, kv_g_src, kv_w_a, kv_g_a, kv_w_b, kv_g_k_nope, kv_g_k_pe, mla_w_q_a, mla_g_q_a, mla_w_q_b, mla_g_q_nope, mla_g_q_pe, mla_w_o, ffn_w_gate, ffn_w_up, ffn_w_down, moe_w_router, moe_b_router, moe_w_gate, moe_w_up, moe_w_down):
    b, s, d = x.shape
    t = b * s
    depth = g_mix.shape[0]
    n_a = sb_w_qkv.shape[0]
    tiles = _tiles(t, s, ffn_w_gate.shape[2], moe_w_gate.shape[3])
    cos_t, sin_t = _rope_tables(s)
    h = x.reshape(t, d)
    hw = N_HEADS * SB_HEAD_DIM
    kcat = vcat = None

    for i in range(depth):
        if i < n_a:
            col_scale = jnp.concatenate([jnp.full((hw,), 1.0 / math.sqrt(SB_HEAD_DIM), F32),
                                         jnp.ones((2 * hw,), F32)])
            w_qkv = (sb_w_qkv[i] * col_scale).astype(BF16)
            qkv = rms_matmul(h, g_mix[i], w_qkv, BF16, tiles["proj_tm"], 512)
            att = sb_attention(qkv.reshape(b, s, 3 * hw), tiles["attn_tq"])
            h = matmul_residual(att.reshape(t, hw), sb_w_o[i].astype(BF16), h, tiles["proj_tm"])
        else:
            j = i - n_a
            if kcat is None:
                pad = jnp.zeros((d, LANES - ROPE_DIM), F32)
                w_a = jnp.concatenate([kv_w_a, pad], axis=1).astype(BF16)
                kva = rms_matmul(h, kv_g_src, w_a, F32, tiles["proj_tm"], w_a.shape[1])
                w_b = kv_w_b.reshape(KV_RANK, N_HEADS, NOPE_DIM + V_DIM)
                wn = w_b[:, :, :NOPE_DIM].reshape(KV_RANK, N_HEADS * NOPE_DIM).astype(BF16)
                wv = w_b[:, :, NOPE_DIM:].reshape(KV_RANK, N_HEADS * V_DIM).astype(BF16)
                g_pe_pad = jnp.concatenate([kv_g_k_pe, jnp.zeros((LANES - ROPE_DIM,), F32)]).reshape(1, LANES)
                kcat, vcat = kv_prep(kva, kv_g_a.reshape(1, KV_RANK), g_pe_pad, cos_t, sin_t, wn, wv,
                                     jnp.tile(kv_g_k_nope, N_HEADS).reshape(1, -1), tiles["prep_tm"], s)
                kcat = kcat.reshape(b, s, -1)
                vcat = vcat.reshape(b, s, -1)
            qa = rms_matmul(h, g_mix[i], mla_w_q_a[j].astype(BF16), F32, tiles["proj_tm"], mla_w_q_a.shape[2])
            w_qb = mla_w_q_b[j].reshape(-1, N_HEADS, NOPE_DIM + ROPE_DIM)
            wqn = w_qb[:, :, :NOPE_DIM].reshape(-1, N_HEADS * NOPE_DIM).astype(BF16)
            wqp = w_qb[:, :, NOPE_DIM:].reshape(-1, N_HEADS * ROPE_DIM).astype(BF16)
            scale = LOG2E / math.sqrt(NOPE_DIM + ROPE_DIM)
            qcat = q_prep(qa, mla_g_q_a[j].reshape(1, -1), wqn, wqp,
                          jnp.tile(mla_g_q_nope[j], N_HEADS).reshape(1, -1),
                          jnp.tile(mla_g_q_pe[j], N_HEADS).reshape(1, -1),
                          cos_t, sin_t, tiles["prep_tm"], s, scale)
            att = mla_attention(qcat.reshape(b, s, -1), kcat, vcat, tiles["attn_tq"])
            h = matmul_residual(att.reshape(t, -1), mla_w_o[j].astype(BF16), h, tiles["proj_tm"])

        if i % 2 == 0:
            m = i // 2
            h = dense_ffn(h, g_ffn[i], ffn_w_gate[m].astype(BF16), ffn_w_up[m].astype(BF16),
                          ffn_w_down[m].astype(BF16), tiles["ffn_tm"], tiles["ffn_tf"])
        else:
            m = i // 2
            h = moe_layer(h, g_ffn[i], moe_w_router[m], moe_b_router[m], moe_w_gate[m].astype(BF16),
                          moe_w_up[m].astype(BF16), moe_w_down[m].astype(BF16), tiles)
    return h.reshape(b, s, d)
```

```python
import functools
import math

import jax
import jax.numpy as jnp
from jax import lax
from jax.experimental import pallas as pl
from jax.experimental.pallas import tpu as pltpu

F32 = jnp.float32
BF16 = jnp.bfloat16
I32 = jnp.int32

EPS = 1e-6
N_HEADS = 16
SB_HEAD_DIM = 64
NOPE_DIM = 64
ROPE_DIM = 32
V_DIM = 64
KV_RANK = 256
ROPE_THETA = 10000.0
N_EXPERTS = 8
LANES = 128
PAIR_W = 2 * NOPE_DIM + LANES
ONES_LANE = 2 * NOPE_DIM + 2 * ROPE_DIM
MLA_STATIC_SHIFT_MAX = 50.0
LOG2E = 1.4426950408889634
SB_SKIP_EXP = 110.0
MXU_TILE = 256
VMEM_LIMIT = 52 * 1024 * 1024
VMEM_LIMIT_BIG = 58 * 1024 * 1024


def _params(sem, vmem=VMEM_LIMIT):
    return pltpu.CompilerParams(dimension_semantics=sem, vmem_limit_bytes=vmem)


def _dot(a, b):
    return jnp.dot(a, b, preferred_element_type=F32)


def _dot_nt(a, b):
    return lax.dot_general(a, b, (((1,), (1,)), ((), ())), preferred_element_type=F32)


def _split_bf16(x):
    hi = x.astype(BF16)
    lo = (x - hi.astype(F32)).astype(BF16)
    return hi, lo


def _dot_hilo(x, w):
    hi, lo = _split_bf16(x)
    return _dot(hi, w) + _dot(lo, w)


def _rms_matmul_body(x_ref, g_ref, w_ref, o_ref, xn_ref):
    @pl.when(pl.program_id(1) == 0)
    def _():
        x = x_ref[...]
        inv = lax.rsqrt(jnp.mean(x * x, axis=-1, keepdims=True) + EPS)
        xn_ref[...] = (x * inv * g_ref[...]).astype(BF16)

    o_ref[...] = _dot(xn_ref[...], w_ref[...]).astype(o_ref.dtype)


def rms_matmul(x, g, w, out_dtype, tm, tn):
    t, d = x.shape
    n = w.shape[1]
    return pl.pallas_call(
        _rms_matmul_body,
        grid=(t // tm, n // tn),
        in_specs=[pl.BlockSpec((tm, d), lambda i, j: (i, 0)),
                  pl.BlockSpec((1, d), lambda i, j: (0, 0)),
                  pl.BlockSpec((d, tn), lambda i, j: (0, j))],
        out_specs=pl.BlockSpec((tm, tn), lambda i, j: (i, j)),
        out_shape=jax.ShapeDtypeStruct((t, n), out_dtype),
        scratch_shapes=[pltpu.VMEM((tm, d), BF16)],
        compiler_params=_params(("parallel", "arbitrary")),
        name="rms_matmul",
    )(x, g.reshape(1, d), w)


def _matmul_res_body(a_ref, w_ref, r_ref, o_ref):
    o_ref[...] = r_ref[...] + _dot(a_ref[...], w_ref[...])


def matmul_residual(a, w, res, tm):
    t, k = a.shape
    n = w.shape[1]
    return pl.pallas_call(
        _matmul_res_body,
        grid=(t // tm,),
        in_specs=[pl.BlockSpec((tm, k), lambda i: (i, 0)),
                  pl.BlockSpec((k, n), lambda i: (0, 0)),
                  pl.BlockSpec((tm, n), lambda i: (i, 0))],
        out_specs=pl.BlockSpec((tm, n), lambda i: (i, 0)),
        out_shape=jax.ShapeDtypeStruct((t, n), F32),
        compiler_params=_params(("parallel",)),
        name="matmul_residual",
    )(a, w, res)


def _softplus2(z):
    return jnp.maximum(jnp.log2(1.0 + jnp.exp2(jnp.minimum(z, 64.0))), z)


def _key_norm_max(k_ref, head_masks, kmax_ref):
    k = k_ref[0].astype(F32)
    k2 = k * k
    for hh, mask in enumerate(head_masks):
        n2 = jnp.sum(jnp.where(mask, k2, 0.0), axis=-1, keepdims=True)
        kmax_ref[hh] = jnp.max(n2, axis=0, keepdims=True)


def _logit_bound(qm, kmax):
    qf = qm.astype(F32)
    return jnp.sqrt(jnp.sum(qf * qf, axis=-1, keepdims=True) * kmax) * 1.01 + 1e-2


def _sb_attn_body(q_ref, k_ref, v_ref, tri_ref, o_ref, kmax_ref, qs_ref, zb_ref, carry_ref, acc_ref,
                  *, tq):
    qi = pl.program_id(2)
    lane = lax.broadcasted_iota(I32, (1, LANES), 1)
    head_masks = [(lane // SB_HEAD_DIM) == hh for hh in range(2)]

    @pl.when(qi == 0)
    def _():
        _key_norm_max(k_ref, head_masks, kmax_ref)

    q2 = q_ref[0]
    for hh in range(2):
        qm = jnp.where(head_masks[hh], q2, jnp.zeros_like(q2))
        qs_ref[hh] = qm
        zb_ref[hh] = _logit_bound(qm, kmax_ref[hh])

    row = lax.broadcasted_iota(I32, (tq, tq), 0)
    col = lax.broadcasted_iota(I32, (tq, tq), 1)
    causal = col < row

    def chain(hh, start, masked):
        z = _dot_nt(qs_ref[hh], k_ref[0, pl.ds(start, tq), :])
        sp = _softplus2(z)
        if masked:
            sp = jnp.where(causal, sp, 0.0)
        hi, lo = _split_bf16(sp)
        csum = _dot(jnp.concatenate([hi, lo], axis=-1), tri_ref[...])
        e = jnp.exp2(z - csum)
        if masked:
            e = jnp.where(causal, e, 0.0)
        return _dot(e.astype(BF16), v_ref[0, pl.ds(start, tq), :]), csum[:, 0:1]

    has_prev = qi >= 1
    prev_start = pl.multiple_of(jnp.maximum(qi - 1, 0) * tq, tq)
    for hh in range(2):
        pv_d, tot_d = chain(hh, pl.multiple_of(qi * tq, tq), True)
        pv_p, tot_p = chain(hh, prev_start, False)
        acc_ref[hh] = pv_d + jnp.where(has_prev, jnp.exp2(-tot_d), 0.0) * pv_p
        carry_ref[hh] = tot_d + jnp.where(has_prev, tot_p, 0.0)

    def cond(jj):
        live = jnp.max(zb_ref[...] - carry_ref[...]) > -SB_SKIP_EXP * LOG2E
        return jnp.logical_and(jj <= qi, live)

    def body(jj):
        start = pl.multiple_of((qi - jj) * tq, tq)
        for hh in range(2):
            pv, tot = chain(hh, start, False)
            carry = carry_ref[hh]
            acc_ref[hh] += jnp.exp2(-carry) * pv
            carry_ref[hh] = carry + tot
        return jj + 1

    lax.while_loop(cond, body, jnp.int32(2))
    o_ref[0] = jnp.where(head_masks[0], acc_ref[0], acc_ref[1]).astype(o_ref.dtype)


def sb_attention(qkv, tq):
    b, s, _ = qkv.shape
    n_pairs = N_HEADS // 2
    ji = lax.broadcasted_iota(I32, (2 * tq, tq), 0) % tq
    si = lax.broadcasted_iota(I32, (2 * tq, tq), 1)
    tri = (ji >= si).astype(BF16)
    return pl.pallas_call(
        functools.partial(_sb_attn_body, tq=tq),
        grid=(b, n_pairs, s // tq),
        in_specs=[pl.BlockSpec((1, tq, LANES), lambda bi, p, i: (bi, i, p)),
                  pl.BlockSpec((1, s, LANES), lambda bi, p, i: (bi, 0, n_pairs + p)),
                  pl.BlockSpec((1, s, LANES), lambda bi, p, i: (bi, 0, 2 * n_pairs + p)),
                  pl.BlockSpec((2 * tq, tq), lambda bi, p, i: (0, 0))],
        out_specs=pl.BlockSpec((1, tq, LANES), lambda bi, p, i: (bi, i, p)),
        out_shape=jax.ShapeDtypeStruct((b, s, N_HEADS * SB_HEAD_DIM), BF16),
        scratch_shapes=[pltpu.VMEM((2, 1, 1), F32),
                        pltpu.VMEM((2, tq, LANES), BF16),
                        pltpu.VMEM((2, tq, 1), F32),
                        pltpu.VMEM((2, tq, 1), F32),
                        pltpu.VMEM((2, tq, LANES), F32)],
        compiler_params=_params(("parallel", "parallel", "arbitrary")),
        name="sb_attention",
    )(qkv, qkv, qkv, tri)


def _mla_attn_body(q_ref, k_ref, v_ref, o_ref, kmax_ref, qs_ref, m_ref, acc_ref, *, tq, tk):
    qi = pl.program_id(2)
    lane_q = lax.broadcasted_iota(I32, (1, PAIR_W), 1)
    lane_o = lax.broadcasted_iota(I32, (1, LANES), 1)
    head_masks = [jnp.logical_or((lane_q // NOPE_DIM) == hh,
                                 jnp.logical_and(lane_q >= 2 * NOPE_DIM,
                                                 ((lane_q - 2 * NOPE_DIM) // ROPE_DIM) == hh))
                  for hh in range(2)]

    @pl.when(qi == 0)
    def _():
        _key_norm_max(k_ref, head_masks, kmax_ref)

    q2 = q_ref[0]
    shift_max = jnp.float32(0.0)
    for hh in range(2):
        qm = jnp.where(head_masks[hh], q2, jnp.zeros_like(q2))
        bound = _logit_bound(qm, kmax_ref[hh])
        qs_ref[hh] = jnp.where(lane_q == ONES_LANE, (-bound).astype(BF16), qm)
        shift_max = jnp.maximum(shift_max, jnp.max(bound))
        acc_ref[hh] = jnp.zeros((tq, 2 * LANES), F32)

    ones = jnp.ones((tk, LANES), BF16)

    def scores(hh, kb, start, masked):
        sc = _dot_nt(qs_ref[hh], kb)
        if masked:
            qpos = qi * tq + lax.broadcasted_iota(I32, (tq, tk), 0)
            kpos = start + lax.broadcasted_iota(I32, (tq, tk), 1)
            sc = jnp.where(kpos <= qpos, sc, -jnp.inf)
        return sc

    def static_block(start, masked):
        kb = k_ref[0, pl.ds(start, tk), :]
        vext = jnp.concatenate([v_ref[0, pl.ds(start, tk), :], ones], axis=-1)
        for hh in range(2):
            p = jnp.exp2(scores(hh, kb, start, masked))
            acc_ref[hh] += _dot(p.astype(BF16), vext)

    def online_block(start, masked):
        kb = k_ref[0, pl.ds(start, tk), :]
        vext = jnp.concatenate([v_ref[0, pl.ds(start, tk), :], ones], axis=-1)
        for hh in range(2):
            sc = scores(hh, kb, start, masked)
            m_prev = m_ref[hh]
            m_new = jnp.maximum(m_prev, jnp.max(sc, axis=-1, keepdims=True))
            p = jnp.exp2(sc - m_new)
            acc_ref[hh] = jnp.exp2(m_prev - m_new) * acc_ref[hh] + _dot(p.astype(BF16), vext)
            m_ref[hh] = m_new

    n_full = (qi * tq) // tk

    def run(block):
        def body(j, c):
            block(pl.multiple_of(j * tk, tk), False)
            return c
        lax.fori_loop(0, n_full, body, 0)
        block(pl.multiple_of(n_full * tk, tk), True)

    @pl.when(shift_max <= MLA_STATIC_SHIFT_MAX)
    def _():
        run(static_block)

    @pl.when(shift_max > MLA_STATIC_SHIFT_MAX)
    def _():
        m_ref[...] = jnp.full_like(m_ref, -jnp.inf)
        run(online_block)

    outs = [acc_ref[hh][:, :LANES] / acc_ref[hh][:, LANES:] for hh in range(2)]
    o_ref[0] = jnp.where(lane_o < V_DIM, outs[0], outs[1]).astype(o_ref.dtype)


def mla_attention(qcat, kcat, v, tq, tk):
    b, s, _ = qcat.shape
    n_pairs = N_HEADS // 2
    return pl.pallas_call(
        functools.partial(_mla_attn_body, tq=tq, tk=tk),
        grid=(b, n_pairs, s // tq),
        in_specs=[pl.BlockSpec((1, tq, PAIR_W), lambda bi, p, i: (bi, i, p)),
                  pl.BlockSpec((1, s, PAIR_W), lambda bi, p, i: (bi, 0, p)),
                  pl.BlockSpec((1, s, LANES), lambda bi, p, i: (bi, 0, p))],
        out_specs=pl.BlockSpec((1, tq, LANES), lambda bi, p, i: (bi, i, p)),
        out_shape=jax.ShapeDtypeStruct((b, s, N_HEADS * V_DIM), BF16),
        scratch_shapes=[pltpu.VMEM((2, 1, 1), F32),
                        pltpu.VMEM((2, tq, PAIR_W), BF16),
                        pltpu.VMEM((2, tq, 1), F32),
                        pltpu.VMEM((2, tq, 2 * LANES), F32)],
        compiler_params=_params(("parallel", "parallel", "arbitrary")),
        name="mla_attention",
    )(qcat, kcat, v)


def _group_rms(x, gmat, group, g):
    chunks = []
    for c in range(x.shape[1] // 256):
        xc = x[:, c * 256:(c + 1) * 256]
        ss = _dot_hilo(xc * xc, gmat)
        chunks.append(xc * lax.rsqrt(ss * (1.0 / group) + EPS))
    y = chunks[0] if len(chunks) == 1 else jnp.concatenate(chunks, axis=-1)
    return y * g


def _rope128(x, cos, sin):
    lane = lax.broadcasted_iota(I32, (1, LANES), 1)
    first_half = (lane % ROPE_DIM) < (ROPE_DIM // 2)
    rot = jnp.where(first_half,
                    -pltpu.roll(x, LANES - ROPE_DIM // 2, 1),
                    pltpu.roll(x, ROPE_DIM // 2, 1))
    return x * cos + rot * sin


def _kv_prep_body(kva_ref, g_a_ref, g_pe_ref, cos_ref, sin_ref, wn_ref, wv_ref, g_n_ref,
                  g64_ref, k_ref, v_ref):
    lane = lax.broadcasted_iota(I32, (1, LANES), 1)
    kva = kva_ref[...]
    c = kva[:, :KV_RANK]
    cn = (c * lax.rsqrt(jnp.mean(c * c, axis=-1, keepdims=True) + EPS) * g_a_ref[...]).astype(BF16)
    pe = kva[:, KV_RANK:KV_RANK + LANES]
    ms = jnp.sum(pe * pe, axis=-1, keepdims=True) * (1.0 / ROPE_DIM)
    pen = pe * lax.rsqrt(ms + EPS) * g_pe_ref[...]
    kpe = _rope128(pen, cos_ref[...], sin_ref[...])
    kpe = jnp.where(lane < ROPE_DIM, kpe, 0.0)
    kpe2 = kpe + pltpu.roll(kpe, ROPE_DIM, 1)
    kpe2 = jnp.where(lane == ONES_LANE - 2 * NOPE_DIM, 1.0, kpe2).astype(BF16)
    kn = _group_rms(_dot(cn, wn_ref[...]), g64_ref[...], NOPE_DIM, g_n_ref[...]).astype(BF16)
    v_ref[...] = _dot(cn, wv_ref[...]).astype(BF16)
    for p in range(N_HEADS // 2):
        k_ref[:, p * PAIR_W:p * PAIR_W + LANES] = kn[:, p * LANES:(p + 1) * LANES]
        k_ref[:, p * PAIR_W + LANES:(p + 1) * PAIR_W] = kpe2


def _q_prep_body(qa_ref, g_a_ref, wn_ref, wp_ref, g_n_ref, g_p_ref, cos_ref, sin_ref,
                 g64_ref, g32_ref, q_ref, *, scale):
    lane = lax.broadcasted_iota(I32, (1, LANES), 1)
    qa = qa_ref[...]
    qan = (qa * lax.rsqrt(jnp.mean(qa * qa, axis=-1, keepdims=True) + EPS) * g_a_ref[...]).astype(BF16)
    qn = _group_rms(_dot(qan, wn_ref[...]), g64_ref[...], NOPE_DIM, g_n_ref[...]) * scale
    qp = _group_rms(_dot(qan, wp_ref[...]), g32_ref[...], ROPE_DIM, g_p_ref[...])
    cos = cos_ref[...]
    sin = sin_ref[...]
    for c in range(N_HEADS * ROPE_DIM // LANES):
        pe4 = _rope128(qp[:, c * LANES:(c + 1) * LANES], cos, sin) * scale
        for half in range(2):
            p = 2 * c + half
            x = pe4 if half == 0 else pltpu.roll(pe4, 2 * ROPE_DIM, 1)
            q_ref[:, p * PAIR_W:p * PAIR_W + LANES] = qn[:, p * LANES:(p + 1) * LANES].astype(BF16)
            q_ref[:, p * PAIR_W + LANES:(p + 1) * PAIR_W] = jnp.where(lane < 2 * ROPE_DIM, x, 0.0).astype(BF16)


def _block_diag_ones(n, group):
    a = lax.broadcasted_iota(I32, (n, n), 0) // group
    b = lax.broadcasted_iota(I32, (n, n), 1) // group
    return (a == b).astype(BF16)


def _full(shape):
    return pl.BlockSpec(shape, lambda i: (0,) * len(shape))


def kv_prep(kva, g_a, g_pe_pad, cos_t, sin_t, wn, wv, g_n_t, tm, seq):
    t = kva.shape[0]
    nseq = seq // tm
    return pl.pallas_call(
        _kv_prep_body,
        grid=(t // tm,),
        in_specs=[pl.BlockSpec((tm, kva.shape[1]), lambda i: (i, 0)),
                  _full((1, KV_RANK)), _full((1, LANES)),
                  pl.BlockSpec((tm, LANES), lambda i: (i % nseq, 0)),
                  pl.BlockSpec((tm, LANES), lambda i: (i % nseq, 0)),
                  _full(wn.shape), _full(wv.shape), _full((1, wn.shape[1])),
                  _full((256, 256))],
        out_specs=[pl.BlockSpec((tm, (N_HEADS // 2) * PAIR_W), lambda i: (i, 0)),
                   pl.BlockSpec((tm, N_HEADS * V_DIM), lambda i: (i, 0))],
        out_shape=[jax.ShapeDtypeStruct((t, (N_HEADS // 2) * PAIR_W), BF16),
                   jax.ShapeDtypeStruct((t, N_HEADS * V_DIM), BF16)],
        compiler_params=_params(("parallel",)),
        name="kv_prep",
    )(kva, g_a, g_pe_pad, cos_t, sin_t, wn, wv, g_n_t, _block_diag_ones(256, NOPE_DIM))


def q_prep(qa, g_a, wn, wp, g_n_t, g_p_t, cos_t, sin_t, tm, seq, scale):
    t = qa.shape[0]
    nseq = seq // tm
    return pl.pallas_call(
        functools.partial(_q_prep_body, scale=scale),
        grid=(t // tm,),
        in_specs=[pl.BlockSpec((tm, qa.shape[1]), lambda i: (i, 0)),
                  _full((1, qa.shape[1])), _full(wn.shape), _full(wp.shape),
                  _full((1, wn.shape[1])), _full((1, wp.shape[1])),
                  pl.BlockSpec((tm, LANES), lambda i: (i % nseq, 0)),
                  pl.BlockSpec((tm, LANES), lambda i: (i % nseq, 0)),
                  _full((256, 256)), _full((256, 256))],
        out_specs=pl.BlockSpec((tm, (N_HEADS // 2) * PAIR_W), lambda i: (i, 0)),
        out_shape=jax.ShapeDtypeStruct((t, (N_HEADS // 2) * PAIR_W), BF16),
        compiler_params=_params(("parallel",)),
        name="q_prep",
    )(qa, g_a, wn, wp, g_n_t, g_p_t, cos_t, sin_t,
      _block_diag_ones(256, NOPE_DIM), _block_diag_ones(256, ROPE_DIM))


def _ffn_body(x_ref, g_ref, wg_ref, wu_ref, wd_ref, o_ref, h_ref):
    x = x_ref[...]
    inv = lax.rsqrt(jnp.mean(x * x, axis=-1, keepdims=True) + EPS)
    xn = (x * inv * g_ref[...]).astype(BF16)
    _swiglu_hidden(xn, wg_ref, wu_ref, h_ref, ())
    o_ref[...] = x + _dot(h_ref[...], wd_ref[...])


def dense_ffn(x, g, wg, wu, wd, tm):
    t, d = x.shape
    f = wg.shape[1]
    once = pl.Buffered(1)
    return pl.pallas_call(
        _ffn_body,
        grid=(t // tm,),
        in_specs=[pl.BlockSpec((tm, d), lambda i: (i, 0)),
                  pl.BlockSpec((1, d), lambda i: (0, 0)),
                  pl.BlockSpec((d, f), lambda i: (0, 0), pipeline_mode=once),
                  pl.BlockSpec((d, f), lambda i: (0, 0), pipeline_mode=once),
                  pl.BlockSpec((f, d), lambda i: (0, 0), pipeline_mode=once)],
        out_specs=pl.BlockSpec((tm, d), lambda i: (i, 0)),
        out_shape=jax.ShapeDtypeStruct((t, d), F32),
        scratch_shapes=[pltpu.VMEM((tm, f), BF16)],
        compiler_params=_params(("parallel",)),
        name="dense_ffn",
    )(x, g.reshape(1, d), wg, wu, wd)


def _router_body(x_ref, g_ref, wr_ref, b_ref, upper_ref,
                 xn_ref, idx_ref, gate_ref, rank_ref, cnt_ref, base_ref):
    i = pl.program_id(0)

    @pl.when(i == 0)
    def _():
        base_ref[...] = jnp.zeros_like(base_ref)

    x = x_ref[...]
    xn = x * lax.rsqrt(jnp.mean(x * x, axis=-1, keepdims=True) + EPS) * g_ref[...]
    xn_ref[...] = xn
    tm = x.shape[0]

    x_hi, x_lo = _split_bf16(xn)
    w_hi, w_lo = _split_bf16(wr_ref[...])
    logits = _dot_nt(w_hi, x_hi) + _dot_nt(w_hi, x_lo) + _dot_nt(w_lo, x_hi) + b_ref[...]

    e_iota = lax.broadcasted_iota(I32, (N_EXPERTS, tm), 0)
    m1 = jnp.max(logits, axis=0, keepdims=True)
    i1 = jnp.min(jnp.where(logits == m1, e_iota, N_EXPERTS), axis=0, keepdims=True)
    sel1 = e_iota == i1
    rest = jnp.where(sel1, -jnp.inf, logits)
    m2 = jnp.max(rest, axis=0, keepdims=True)
    i2 = jnp.min(jnp.where(rest == m2, e_iota, N_EXPERTS), axis=0, keepdims=True)
    sel2 = e_iota == i2
    e2 = jnp.exp(m2 - m1)
    g1 = 1.0 / (1.0 + e2)
    idx_ref[...] = jnp.concatenate([i1, i2], axis=0)
    gate_ref[...] = jnp.concatenate([g1, e2 * g1], axis=0)

    member = jnp.logical_or(sel1, sel2)
    prefix = _dot(member.astype(BF16), upper_ref[...])
    rank = prefix + base_ref[...]
    r1 = jnp.sum(jnp.where(sel1, rank, 0.0), axis=0, keepdims=True)
    r2 = jnp.sum(jnp.where(sel2, rank, 0.0), axis=0, keepdims=True)
    rank_ref[...] = jnp.concatenate([r1, r2], axis=0).astype(I32)
    base_ref[...] += jnp.sum(member.astype(F32), axis=1, keepdims=True)
    cnt_ref[...] = jnp.broadcast_to(base_ref[...], cnt_ref.shape).astype(I32)


def moe_router(x, g, w_router, b_router, tm):
    t, d = x.shape
    a = lax.broadcasted_iota(I32, (tm, tm), 0)
    b = lax.broadcasted_iota(I32, (tm, tm), 1)
    upper = (a < b).astype(BF16)
    return pl.pallas_call(
        _router_body,
        grid=(t // tm,),
        in_specs=[pl.BlockSpec((tm, d), lambda i: (i, 0)),
                  _full((1, d)), _full((N_EXPERTS, d)), _full((N_EXPERTS, 1)), _full((tm, tm))],
        out_specs=[pl.BlockSpec((tm, d), lambda i: (i, 0)),
                   pl.BlockSpec((2, tm), lambda i: (0, i)),
                   pl.BlockSpec((2, tm), lambda i: (0, i)),
                   pl.BlockSpec((2, tm), lambda i: (0, i)),
                   _full((N_EXPERTS, LANES))],
        out_shape=[jax.ShapeDtypeStruct((t, d), F32),
                   jax.ShapeDtypeStruct((2, t), I32),
                   jax.ShapeDtypeStruct((2, t), F32),
                   jax.ShapeDtypeStruct((2, t), I32),
                   jax.ShapeDtypeStruct((N_EXPERTS, LANES), I32)],
        scratch_shapes=[pltpu.VMEM((N_EXPERTS, 1), F32)],
        compiler_params=_params(("arbitrary",)),
        name="moe_router",
    )(x, g.reshape(1, d), w_router.T, b_router.reshape(N_EXPERTS, 1), upper)


def _row_copy(src, src_row, dst, dst_row, sem):
    return pltpu.make_async_copy(src.at[pl.ds(src_row, 1)], dst.at[pl.ds(dst_row, 1)], sem)


def _dispatch_body(dest_ref, xn_ref, buf_in_hbm, buf_hbm, sem, *, tb, t_total):
    del buf_in_hbm
    base = pl.program_id(0) * tb

    def issue(t, c):
        for k in range(2):
            _row_copy(xn_ref, t, buf_hbm, dest_ref[k * t_total + base + t], sem).start()
        return c

    lax.fori_loop(0, tb, issue, 0, unroll=8)

    def drain(t, c):
        for k in range(2):
            _row_copy(xn_ref, 0, buf_hbm, 0, sem).wait()
        return c

    lax.fori_loop(0, tb, drain, 0, unroll=8)


def moe_dispatch(dest_flat, xn, n_rows, tb):
    t, d = xn.shape
    buf0 = jnp.zeros((n_rows, d), xn.dtype)
    grid_spec = pltpu.PrefetchScalarGridSpec(
        num_scalar_prefetch=1,
        grid=(t // tb,),
        in_specs=[pl.BlockSpec((tb, d), lambda i, dest: (i, 0)), pl.BlockSpec(memory_space=pl.ANY)],
        out_specs=pl.BlockSpec(memory_space=pl.ANY),
        scratch_shapes=[pltpu.SemaphoreType.DMA(())],
    )
    return pl.pallas_call(
        functools.partial(_dispatch_body, tb=tb, t_total=t),
        grid_spec=grid_spec,
        out_shape=jax.ShapeDtypeStruct((n_rows, d), xn.dtype),
        input_output_aliases={2: 0},
        compiler_params=_params(("arbitrary",)),
        name="moe_dispatch",
    )(dest_flat, xn, buf0)


def _swiglu_hidden(xb, wg_ref, wu_ref, h_ref, widx):
    f = h_ref.shape[1]
    for c0 in range(0, f, MXU_TILE):
        c1 = min(c0 + MXU_TILE, f)
        gate = _dot(xb, wg_ref[widx + (slice(None), slice(c0, c1))])
        up = _dot(xb, wu_ref[widx + (slice(None), slice(c0, c1))])
        h_ref[:, c0:c1] = (gate * jax.nn.sigmoid(gate) * up).astype(BF16)


def _experts_body(be_ref, nv_ref, x_ref, wg_ref, wu_ref, wd_ref, o_ref, h_ref):
    del be_ref
    valid = pl.program_id(0) < nv_ref[0]

    @pl.when(valid)
    def _():
        _swiglu_hidden(x_ref[...].astype(BF16), wg_ref, wu_ref, h_ref, (0,))
        o_ref[...] = _dot(h_ref[...], wd_ref[0])

    @pl.when(jnp.logical_not(valid))
    def _():
        o_ref[...] = jnp.zeros_like(o_ref)


def moe_experts(block_e, n_valid, buf, wg, wu, wd, tm):
    n_rows, d = buf.shape
    f = wg.shape[2]
    once = pl.Buffered(1)
    grid_spec = pltpu.PrefetchScalarGridSpec(
        num_scalar_prefetch=2,
        grid=(n_rows // tm,),
        in_specs=[pl.BlockSpec((tm, d), lambda i, be, nv: (i, 0)),
                  pl.BlockSpec((1, d, f), lambda i, be, nv: (be[i], 0, 0), pipeline_mode=once),
                  pl.BlockSpec((1, d, f), lambda i, be, nv: (be[i], 0, 0), pipeline_mode=once),
                  pl.BlockSpec((1, f, d), lambda i, be, nv: (be[i], 0, 0), pipeline_mode=once)],
        out_specs=pl.BlockSpec((tm, d), lambda i, be, nv: (i, 0)),
        scratch_shapes=[pltpu.VMEM((tm, f), BF16)],
    )
    return pl.pallas_call(
        _experts_body,
        grid_spec=grid_spec,
        out_shape=jax.ShapeDtypeStruct((n_rows, d), F32),
        compiler_params=_params(("arbitrary",), vmem=VMEM_LIMIT_BIG),
        name="moe_experts",
    )(block_e, n_valid, buf, wg, wu, wd)


def _combine_body(dest_ref, x_ref, gate_ref, eo_hbm, o_ref, buf_ref, sem, *, tb, t_total):
    base = pl.program_id(0) * tb

    def issue(t, c):
        for k in range(2):
            _row_copy(eo_hbm, dest_ref[k * t_total + base + t], buf_ref.at[k], t, sem).start()
        return c

    lax.fori_loop(0, tb, issue, 0, unroll=8)

    def drain(t, c):
        for k in range(2):
            _row_copy(eo_hbm, 0, buf_ref.at[k], 0, sem).wait()
        return c

    lax.fori_loop(0, tb, drain, 0, unroll=8)
    gates = gate_ref[...]
    o_ref[...] = x_ref[...] + gates[:, 0:1] * buf_ref[0] + gates[:, 1:2] * buf_ref[1]


def moe_combine(dest_flat, x, gates_t, expert_out, tb):
    t, d = x.shape
    grid_spec = pltpu.PrefetchScalarGridSpec(
        num_scalar_prefetch=1,
        grid=(t // tb,),
        in_specs=[pl.BlockSpec((tb, d), lambda i, dest: (i, 0)),
                  pl.BlockSpec((tb, 2), lambda i, dest: (i, 0)),
                  pl.BlockSpec(memory_space=pl.ANY)],
        out_specs=pl.BlockSpec((tb, d), lambda i, dest: (i, 0)),
        scratch_shapes=[pltpu.VMEM((2, tb, d), F32), pltpu.SemaphoreType.DMA(())],
    )
    return pl.pallas_call(
        functools.partial(_combine_body, tb=tb, t_total=t),
        grid_spec=grid_spec,
        out_shape=jax.ShapeDtypeStruct((t, d), F32),
        compiler_params=_params(("arbitrary",)),
        name="moe_combine",
    )(dest_flat, x, gates_t, expert_out)


def moe_layer(x, g, w_router, b_router, wg, wu, wd, tiles):
    t, d = x.shape
    tm = tiles["moe_tm"]
    xn, idx, gates, rank, cnt = moe_router(x, g, w_router, b_router, tiles["router_tm"])
    counts = cnt[:, 0]
    padded = ((counts + tm - 1) // tm) * tm
    pends = jnp.cumsum(padded)
    pstarts = pends - padded
    group_start = sum(jnp.where(idx == e, pstarts[e], 0) for e in range(N_EXPERTS))
    dest_flat = (group_start + rank).reshape(-1)
    n_rows = 2 * t + N_EXPERTS * tm
    n_blocks = n_rows // tm
    block_row = jnp.arange(n_blocks, dtype=I32) * tm
    block_e = jnp.minimum(sum((block_row >= pends[e]).astype(I32) for e in range(N_EXPERTS)),
                          N_EXPERTS - 1)
    n_valid = (pends[-1:] // tm).astype(I32)
    buf = moe_dispatch(dest_flat, xn, n_rows, tiles["moe_tb"])
    eo = moe_experts(block_e, n_valid, buf, wg, wu, wd, tm)
    return moe_combine(dest_flat, x, gates.T, eo, tiles["moe_tb"])


def _tiles(t, s):
    def fit(n, want):
        while n % want:
            want //= 2
        return want
    return dict(
        proj_tm=fit(t, 1024), attn_tq=fit(s, 256), mla_tk=fit(s, 512), prep_tm=fit(s, 512),
        ffn_tm=fit(t, 512), router_tm=fit(t, 512), moe_tm=fit(t, 512), moe_tb=fit(t, 256),
    )


def _rope_tables(seq):
    pos = jnp.arange(seq, dtype=F32)
    inv_freq = ROPE_THETA ** (-jnp.arange(0, ROPE_DIM, 2, dtype=F32) / ROPE_DIM)
    ang = pos[:, None] * inv_freq[None, :]
    ang = jnp.concatenate([ang, ang], axis=-1)
    reps = LANES // ROPE_DIM
    return jnp.tile(jnp.cos(ang), (1, reps)), jnp.tile(jnp.sin(ang), (1, reps))


def kernel(x, g_mix, g_ffn, sb_w_qkv, sb_w_o, kv_g_src, kv_w_a, kv_g_a, kv_w_b, kv_g_k_nope, kv_g_k_pe, mla_w_q_a, mla_g_q_a, mla_w_q_b, mla_g_q_nope, mla_g_q_pe, mla_w_o, ffn_w_gate, ffn_w_up, ffn_w_down, moe_w_router, moe_b_router, moe_w_gate, moe_w_up, moe_w_down):
    b, s, d = x.shape
    t = b * s
    depth = g_mix.shape[0]
    n_a = sb_w_qkv.shape[0]
    tiles = _tiles(t, s)
    cos_t, sin_t = _rope_tables(s)
    h = x.reshape(t, d)
    hw = N_HEADS * SB_HEAD_DIM
    kcat = vcat = None

    for i in range(depth):
        if i < n_a:
            col_scale = jnp.concatenate([jnp.full((hw,), LOG2E / math.sqrt(SB_HEAD_DIM), F32),
                                         jnp.ones((2 * hw,), F32)])
            w_qkv = (sb_w_qkv[i] * col_scale).astype(BF16)
            qkv = rms_matmul(h, g_mix[i], w_qkv, BF16, tiles["proj_tm"], 512)
            att = sb_attention(qkv.reshape(b, s, 3 * hw), tiles["attn_tq"])
            h = matmul_residual(att.reshape(t, hw), sb_w_o[i].astype(BF16), h, tiles["proj_tm"])
        else:
            j = i - n_a
            if kcat is None:
                pad = jnp.zeros((d, LANES - ROPE_DIM), F32)
                w_a = jnp.concatenate([kv_w_a, pad], axis=1).astype(BF16)
                kva = rms_matmul(h, kv_g_src, w_a, F32, tiles["proj_tm"], w_a.shape[1])
                w_b = kv_w_b.reshape(KV_RANK, N_HEADS, NOPE_DIM + V_DIM)
                wn = w_b[:, :, :NOPE_DIM].reshape(KV_RANK, N_HEADS * NOPE_DIM).astype(BF16)
                wv = w_b[:, :, NOPE_DIM:].reshape(KV_RANK, N_HEADS * V_DIM).astype(BF16)
                g_pe_pad = jnp.concatenate([kv_g_k_pe, jnp.zeros((LANES - ROPE_DIM,), F32)]).reshape(1, LANES)
                kcat, vcat = kv_prep(kva, kv_g_a.reshape(1, KV_RANK), g_pe_pad, cos_t, sin_t, wn, wv,
                                     jnp.tile(kv_g_k_nope, N_HEADS).reshape(1, -1), tiles["prep_tm"], s)
                kcat = kcat.reshape(b, s, -1)
                vcat = vcat.reshape(b, s, -1)
            qa = rms_matmul(h, g_mix[i], mla_w_q_a[j].astype(BF16), F32, tiles["proj_tm"], mla_w_q_a.shape[2])
            w_qb = mla_w_q_b[j].reshape(-1, N_HEADS, NOPE_DIM + ROPE_DIM)
            wqn = w_qb[:, :, :NOPE_DIM].reshape(-1, N_HEADS * NOPE_DIM).astype(BF16)
            wqp = w_qb[:, :, NOPE_DIM:].reshape(-1, N_HEADS * ROPE_DIM).astype(BF16)
            scale = LOG2E / math.sqrt(NOPE_DIM + ROPE_DIM)
            qcat = q_prep(qa, mla_g_q_a[j].reshape(1, -1), wqn, wqp,
                          jnp.tile(mla_g_q_nope[j], N_HEADS).reshape(1, -1),
                          jnp.tile(mla_g_q_pe[j], N_HEADS).reshape(1, -1),
                          cos_t, sin_t, tiles["prep_tm"], s, scale)
            att = mla_attention(qcat.reshape(b, s, -1), kcat, vcat, tiles["attn_tq"], tiles["mla_tk"])
            h = matmul_residual(att.reshape(t, -1), mla_w_o[j].astype(BF16), h, tiles["proj_tm"])

        if i % 2 == 0:
            m = i // 2
            h = dense_ffn(h, g_ffn[i], ffn_w_gate[m].astype(BF16), ffn_w_up[m].astype(BF16),
                          ffn_w_down[m].astype(BF16), tiles["ffn_tm"])
        else:
            m = i // 2
            h = moe_layer(h, g_ffn[i], moe_w_router[m], moe_b_router[m], moe_w_gate[m].astype(BF16),
                          moe_w_up[m].astype(BF16), moe_w_down[m].astype(BF16), tiles)
    return h.reshape(b, s, d)
```

```python
import functools
import math

import jax
import jax.numpy as jnp
from jax import lax
from jax.experimental import pallas as pl
from jax.experimental.pallas import tpu as pltpu

F32 = jnp.float32
BF16 = jnp.bfloat16
I32 = jnp.int32

EPS = 1e-6
N_HEADS = 16
SB_HEAD_DIM = 64
NOPE_DIM = 64
ROPE_DIM = 32
V_DIM = 64
KV_RANK = 256
ROPE_THETA = 10000.0
N_EXPERTS = 8
LANES = 128
PAIR_W = 2 * NOPE_DIM + LANES
ONES_LANE = 2 * NOPE_DIM + 2 * ROPE_DIM
MLA_STATIC_SHIFT_MAX = 50.0
LOG2E = 1.4426950408889634
SB_SKIP_EXP = 110.0
MXU_TILE = 256
VMEM_LIMIT = 52 * 1024 * 1024
VMEM_LIMIT_BIG = 58 * 1024 * 1024


def _params(sem, vmem=VMEM_LIMIT):
    return pltpu.CompilerParams(dimension_semantics=sem, vmem_limit_bytes=vmem)


def _dot(a, b):
    return jnp.dot(a, b, preferred_element_type=F32)


def _dot_nt(a, b):
    return lax.dot_general(a, b, (((1,), (1,)), ((), ())), preferred_element_type=F32)


def _split_bf16(x):
    hi = x.astype(BF16)
    lo = (x - hi.astype(F32)).astype(BF16)
    return hi, lo


def _dot_hilo(x, w):
    hi, lo = _split_bf16(x)
    return _dot(hi, w) + _dot(lo, w)


def _rms_matmul_body(x_ref, g_ref, w_ref, o_ref, xn_ref):
    @pl.when(pl.program_id(1) == 0)
    def _():
        x = x_ref[...]
        inv = lax.rsqrt(jnp.mean(x * x, axis=-1, keepdims=True) + EPS)
        xn_ref[...] = (x * inv * g_ref[...]).astype(BF16)

    o_ref[...] = _dot(xn_ref[...], w_ref[...]).astype(o_ref.dtype)


def rms_matmul(x, g, w, out_dtype, tm, tn):
    t, d = x.shape
    n = w.shape[1]
    w_mode = pl.Buffered(1) if tn == n else None
    return pl.pallas_call(
        _rms_matmul_body,
        grid=(t // tm, n // tn),
        in_specs=[pl.BlockSpec((tm, d), lambda i, j: (i, 0)),
                  pl.BlockSpec((1, d), lambda i, j: (0, 0)),
                  pl.BlockSpec((d, tn), lambda i, j: (0, j), pipeline_mode=w_mode)],
        out_specs=pl.BlockSpec((tm, tn), lambda i, j: (i, j)),
        out_shape=jax.ShapeDtypeStruct((t, n), out_dtype),
        scratch_shapes=[pltpu.VMEM((tm, d), BF16)],
        compiler_params=_params(("parallel", "arbitrary")),
        name="rms_matmul",
    )(x, g.reshape(1, d), w)


def _matmul_res_body(a_ref, w_ref, r_ref, o_ref):
    o_ref[...] = r_ref[...] + _dot(a_ref[...], w_ref[...])


def matmul_residual(a, w, res, tm):
    t, k = a.shape
    n = w.shape[1]
    return pl.pallas_call(
        _matmul_res_body,
        grid=(t // tm,),
        in_specs=[pl.BlockSpec((tm, k), lambda i: (i, 0)),
                  pl.BlockSpec((k, n), lambda i: (0, 0)),
                  pl.BlockSpec((tm, n), lambda i: (i, 0))],
        out_specs=pl.BlockSpec((tm, n), lambda i: (i, 0)),
        out_shape=jax.ShapeDtypeStruct((t, n), F32),
        compiler_params=_params(("parallel",)),
        name="matmul_residual",
    )(a, w, res)


def _softplus2(z):
    return jnp.maximum(jnp.log2(1.0 + jnp.exp2(jnp.minimum(z, 64.0))), z)


def _key_norm_max(k_ref, head_masks, kmax_ref):
    k = k_ref[0].astype(F32)
    k2 = k * k
    for hh, mask in enumerate(head_masks):
        n2 = jnp.sum(jnp.where(mask, k2, 0.0), axis=-1, keepdims=True)
        kmax_ref[hh] = jnp.max(n2, axis=0, keepdims=True)


def _logit_bound(qm, kmax):
    qf = qm.astype(F32)
    return jnp.sqrt(jnp.sum(qf * qf, axis=-1, keepdims=True) * kmax) * 1.01 + 1e-2


def _sb_attn_body(q_ref, k_ref, v_ref, tri_ref, o_ref, kmax_ref, qs_ref, zb_ref, carry_ref, acc_ref,
                  *, tq):
    qi = pl.program_id(2)
    lane = lax.broadcasted_iota(I32, (1, LANES), 1)
    head_masks = [(lane // SB_HEAD_DIM) == hh for hh in range(2)]

    @pl.when(qi == 0)
    def _():
        _key_norm_max(k_ref, head_masks, kmax_ref)

    q2 = q_ref[0]
    for hh in range(2):
        qm = jnp.where(head_masks[hh], q2, jnp.zeros_like(q2))
        qs_ref[hh] = qm
        zb_ref[hh] = _logit_bound(qm, kmax_ref[hh])

    row = lax.broadcasted_iota(I32, (tq, tq), 0)
    col = lax.broadcasted_iota(I32, (tq, tq), 1)
    causal = col < row

    def chain(hh, start, masked):
        z = _dot_nt(qs_ref[hh], k_ref[0, pl.ds(start, tq), :])
        sp = _softplus2(z)
        if masked:
            sp = jnp.where(causal, sp, 0.0)
        hi, lo = _split_bf16(sp)
        csum = _dot(jnp.concatenate([hi, lo], axis=-1), tri_ref[...])
        e = jnp.exp2(z - csum)
        if masked:
            e = jnp.where(causal, e, 0.0)
        return _dot(e.astype(BF16), v_ref[0, pl.ds(start, tq), :]), csum[:, 0:1]

    has_prev = qi >= 1
    prev_start = pl.multiple_of(jnp.maximum(qi - 1, 0) * tq, tq)
    for hh in range(2):
        pv_d, tot_d = chain(hh, pl.multiple_of(qi * tq, tq), True)
        pv_p, tot_p = chain(hh, prev_start, False)
        acc_ref[hh] = pv_d + jnp.where(has_prev, jnp.exp2(-tot_d), 0.0) * pv_p
        carry_ref[hh] = tot_d + jnp.where(has_prev, tot_p, 0.0)

    def cond(jj):
        live = jnp.max(zb_ref[...] - carry_ref[...]) > -SB_SKIP_EXP * LOG2E
        return jnp.logical_and(jj <= qi, live)

    def body(jj):
        start = pl.multiple_of((qi - jj) * tq, tq)
        for hh in range(2):
            pv, tot = chain(hh, start, False)
            carry = carry_ref[hh]
            acc_ref[hh] += jnp.exp2(-carry) * pv
            carry_ref[hh] = carry + tot
        return jj + 1

    lax.while_loop(cond, body, jnp.int32(2))
    o_ref[0] = jnp.where(head_masks[0], acc_ref[0], acc_ref[1]).astype(o_ref.dtype)


def sb_attention(qkv, tq):
    b, s, _ = qkv.shape
    n_pairs = N_HEADS // 2
    ji = lax.broadcasted_iota(I32, (2 * tq, tq), 0) % tq
    si = lax.broadcasted_iota(I32, (2 * tq, tq), 1)
    tri = (ji >= si).astype(BF16)
    return pl.pallas_call(
        functools.partial(_sb_attn_body, tq=tq),
        grid=(b, n_pairs, s // tq),
        in_specs=[pl.BlockSpec((1, tq, LANES), lambda bi, p, i: (bi, i, p)),
                  pl.BlockSpec((1, s, LANES), lambda bi, p, i: (bi, 0, n_pairs + p)),
                  pl.BlockSpec((1, s, LANES), lambda bi, p, i: (bi, 0, 2 * n_pairs + p)),
                  pl.BlockSpec((2 * tq, tq), lambda bi, p, i: (0, 0))],
        out_specs=pl.BlockSpec((1, tq, LANES), lambda bi, p, i: (bi, i, p)),
        out_shape=jax.ShapeDtypeStruct((b, s, N_HEADS * SB_HEAD_DIM), BF16),
        scratch_shapes=[pltpu.VMEM((2, 1, 1), F32),
                        pltpu.VMEM((2, tq, LANES), BF16),
                        pltpu.VMEM((2, tq, 1), F32),
                        pltpu.VMEM((2, tq, 1), F32),
                        pltpu.VMEM((2, tq, LANES), F32)],
        compiler_params=_params(("parallel", "parallel", "arbitrary")),
        name="sb_attention",
    )(qkv, qkv, qkv, tri)


def _mla_attn_body(q_ref, k_ref, v_ref, o_ref, kmax_ref, qs_ref, m_ref, acc_ref, *, tq):
    qi = pl.program_id(2)
    lane_q = lax.broadcasted_iota(I32, (1, PAIR_W), 1)
    lane_o = lax.broadcasted_iota(I32, (1, LANES), 1)
    head_masks = [jnp.logical_or((lane_q // NOPE_DIM) == hh,
                                 jnp.logical_and(lane_q >= 2 * NOPE_DIM,
                                                 ((lane_q - 2 * NOPE_DIM) // ROPE_DIM) == hh))
                  for hh in range(2)]

    @pl.when(qi == 0)
    def _():
        _key_norm_max(k_ref, head_masks, kmax_ref)

    q2 = q_ref[0]
    shift_max = jnp.float32(0.0)
    for hh in range(2):
        qm = jnp.where(head_masks[hh], q2, jnp.zeros_like(q2))
        bound = _logit_bound(qm, kmax_ref[hh])
        qs_ref[hh] = jnp.where(lane_q == ONES_LANE, (-bound).astype(BF16), qm)
        shift_max = jnp.maximum(shift_max, jnp.max(bound))
        acc_ref[hh] = jnp.zeros((tq, 2 * LANES), F32)

    ones = jnp.ones((tq, LANES), BF16)
    causal = (lax.broadcasted_iota(I32, (tq, tq), 1) <= lax.broadcasted_iota(I32, (tq, tq), 0))

    def scores(hh, j, masked):
        sc = _dot_nt(qs_ref[hh], k_ref[0, pl.ds(pl.multiple_of(j * tq, tq), tq), :])
        return jnp.where(causal, sc, -jnp.inf) if masked else sc

    def values(j):
        return jnp.concatenate([v_ref[0, pl.ds(pl.multiple_of(j * tq, tq), tq), :], ones], axis=-1)

    def static_blocks(blocks):
        for hh in range(2):
            acc_ref[hh] += sum(_dot(jnp.exp2(scores(hh, j, masked)).astype(BF16), values(j))
                               for j, masked in blocks)

    def online_block(j, masked):
        for hh in range(2):
            sc = scores(hh, j, masked)
            m_prev = m_ref[hh]
            m_new = jnp.maximum(m_prev, jnp.max(sc, axis=-1, keepdims=True))
            p = jnp.exp2(sc - m_new)
            acc_ref[hh] = jnp.exp2(m_prev - m_new) * acc_ref[hh] + _dot(p.astype(BF16), values(j))
            m_ref[hh] = m_new

    @pl.when(shift_max <= MLA_STATIC_SHIFT_MAX)
    def _():
        def pair(j, c):
            static_blocks([(2 * j, False), (2 * j + 1, False)])
            return c
        lax.fori_loop(0, qi // 2, pair, 0)

        @pl.when(qi % 2 == 1)
        def _():
            static_blocks([(qi - 1, False), (qi, True)])

        @pl.when(qi % 2 == 0)
        def _():
            static_blocks([(qi, True)])

    @pl.when(shift_max > MLA_STATIC_SHIFT_MAX)
    def _():
        m_ref[...] = jnp.full_like(m_ref, -jnp.inf)

        def body(j, c):
            online_block(j, False)
            return c
        lax.fori_loop(0, qi, body, 0)
        online_block(qi, True)

    outs = [acc_ref[hh][:, :LANES] / acc_ref[hh][:, LANES:] for hh in range(2)]
    o_ref[0] = jnp.where(lane_o < V_DIM, outs[0], outs[1]).astype(o_ref.dtype)


def mla_attention(qcat, kcat, v, tq):
    b, s, _ = qcat.shape
    n_pairs = N_HEADS // 2
    return pl.pallas_call(
        functools.partial(_mla_attn_body, tq=tq),
        grid=(b, n_pairs, s // tq),
        in_specs=[pl.BlockSpec((1, tq, PAIR_W), lambda bi, p, i: (bi, i, p)),
                  pl.BlockSpec((1, s, PAIR_W), lambda bi, p, i: (bi, 0, p)),
                  pl.BlockSpec((1, s, LANES), lambda bi, p, i: (bi, 0, p))],
        out_specs=pl.BlockSpec((1, tq, LANES), lambda bi, p, i: (bi, i, p)),
        out_shape=jax.ShapeDtypeStruct((b, s, N_HEADS * V_DIM), BF16),
        scratch_shapes=[pltpu.VMEM((2, 1, 1), F32),
                        pltpu.VMEM((2, tq, PAIR_W), BF16),
                        pltpu.VMEM((2, tq, 1), F32),
                        pltpu.VMEM((2, tq, 2 * LANES), F32)],
        compiler_params=_params(("parallel", "parallel", "arbitrary")),
        name="mla_attention",
    )(qcat, kcat, v)


def _group_rms(x, gmat, group, g):
    chunks = []
    for c in range(x.shape[1] // 256):
        xc = x[:, c * 256:(c + 1) * 256]
        ss = _dot_hilo(xc * xc, gmat)
        chunks.append(xc * lax.rsqrt(ss * (1.0 / group) + EPS))
    y = chunks[0] if len(chunks) == 1 else jnp.concatenate(chunks, axis=-1)
    return y * g


def _rope128(x, cos, sin):
    lane = lax.broadcasted_iota(I32, (1, LANES), 1)
    first_half = (lane % ROPE_DIM) < (ROPE_DIM // 2)
    rot = jnp.where(first_half,
                    -pltpu.roll(x, LANES - ROPE_DIM // 2, 1),
                    pltpu.roll(x, ROPE_DIM // 2, 1))
    return x * cos + rot * sin


def _kv_prep_body(kva_ref, g_a_ref, g_pe_ref, cos_ref, sin_ref, wn_ref, wv_ref, g_n_ref,
                  g64_ref, k_ref, v_ref):
    lane = lax.broadcasted_iota(I32, (1, LANES), 1)
    kva = kva_ref[...]
    c = kva[:, :KV_RANK]
    cn = (c * lax.rsqrt(jnp.mean(c * c, axis=-1, keepdims=True) + EPS) * g_a_ref[...]).astype(BF16)
    pe = kva[:, KV_RANK:KV_RANK + LANES]
    ms = jnp.sum(pe * pe, axis=-1, keepdims=True) * (1.0 / ROPE_DIM)
    pen = pe * lax.rsqrt(ms + EPS) * g_pe_ref[...]
    kpe = _rope128(pen, cos_ref[...], sin_ref[...])
    kpe = jnp.where(lane < ROPE_DIM, kpe, 0.0)
    kpe2 = kpe + pltpu.roll(kpe, ROPE_DIM, 1)
    kpe2 = jnp.where(lane == ONES_LANE - 2 * NOPE_DIM, 1.0, kpe2).astype(BF16)
    kn = _group_rms(_dot(cn, wn_ref[...]), g64_ref[...], NOPE_DIM, g_n_ref[...]).astype(BF16)
    v_ref[...] = _dot(cn, wv_ref[...]).astype(BF16)
    for p in range(N_HEADS // 2):
        k_ref[:, p * PAIR_W:p * PAIR_W + LANES] = kn[:, p * LANES:(p + 1) * LANES]
        k_ref[:, p * PAIR_W + LANES:(p + 1) * PAIR_W] = kpe2


def _q_prep_body(qa_ref, g_a_ref, wn_ref, wp_ref, g_n_ref, g_p_ref, cos_ref, sin_ref,
                 g64_ref, g32_ref, q_ref, *, scale):
    lane = lax.broadcasted_iota(I32, (1, LANES), 1)
    qa = qa_ref[...]
    qan = (qa * lax.rsqrt(jnp.mean(qa * qa, axis=-1, keepdims=True) + EPS) * g_a_ref[...]).astype(BF16)
    qn = _group_rms(_dot(qan, wn_ref[...]), g64_ref[...], NOPE_DIM, g_n_ref[...]) * scale
    qp = _group_rms(_dot(qan, wp_ref[...]), g32_ref[...], ROPE_DIM, g_p_ref[...])
    cos = cos_ref[...]
    sin = sin_ref[...]
    for c in range(N_HEADS * ROPE_DIM // LANES):
        pe4 = _rope128(qp[:, c * LANES:(c + 1) * LANES], cos, sin) * scale
        for half in range(2):
            p = 2 * c + half
            x = pe4 if half == 0 else pltpu.roll(pe4, 2 * ROPE_DIM, 1)
            q_ref[:, p * PAIR_W:p * PAIR_W + LANES] = qn[:, p * LANES:(p + 1) * LANES].astype(BF16)
            q_ref[:, p * PAIR_W + LANES:(p + 1) * PAIR_W] = jnp.where(lane < 2 * ROPE_DIM, x, 0.0).astype(BF16)


def _block_diag_ones(n, group):
    a = lax.broadcasted_iota(I32, (n, n), 0) // group
    b = lax.broadcasted_iota(I32, (n, n), 1) // group
    return (a == b).astype(BF16)


def _full(shape):
    return pl.BlockSpec(shape, lambda i: (0,) * len(shape))


def kv_prep(kva, g_a, g_pe_pad, cos_t, sin_t, wn, wv, g_n_t, tm, seq):
    t = kva.shape[0]
    nseq = seq // tm
    return pl.pallas_call(
        _kv_prep_body,
        grid=(t // tm,),
        in_specs=[pl.BlockSpec((tm, kva.shape[1]), lambda i: (i, 0)),
                  _full((1, KV_RANK)), _full((1, LANES)),
                  pl.BlockSpec((tm, LANES), lambda i: (i % nseq, 0)),
                  pl.BlockSpec((tm, LANES), lambda i: (i % nseq, 0)),
                  _full(wn.shape), _full(wv.shape), _full((1, wn.shape[1])),
                  _full((256, 256))],
        out_specs=[pl.BlockSpec((tm, (N_HEADS // 2) * PAIR_W), lambda i: (i, 0)),
                   pl.BlockSpec((tm, N_HEADS * V_DIM), lambda i: (i, 0))],
        out_shape=[jax.ShapeDtypeStruct((t, (N_HEADS // 2) * PAIR_W), BF16),
                   jax.ShapeDtypeStruct((t, N_HEADS * V_DIM), BF16)],
        compiler_params=_params(("parallel",)),
        name="kv_prep",
    )(kva, g_a, g_pe_pad, cos_t, sin_t, wn, wv, g_n_t, _block_diag_ones(256, NOPE_DIM))


def q_prep(qa, g_a, wn, wp, g_n_t, g_p_t, cos_t, sin_t, tm, seq, scale):
    t = qa.shape[0]
    nseq = seq // tm
    return pl.pallas_call(
        functools.partial(_q_prep_body, scale=scale),
        grid=(t // tm,),
        in_specs=[pl.BlockSpec((tm, qa.shape[1]), lambda i: (i, 0)),
                  _full((1, qa.shape[1])), _full(wn.shape), _full(wp.shape),
                  _full((1, wn.shape[1])), _full((1, wp.shape[1])),
                  pl.BlockSpec((tm, LANES), lambda i: (i % nseq, 0)),
                  pl.BlockSpec((tm, LANES), lambda i: (i % nseq, 0)),
                  _full((256, 256)), _full((256, 256))],
        out_specs=pl.BlockSpec((tm, (N_HEADS // 2) * PAIR_W), lambda i: (i, 0)),
        out_shape=jax.ShapeDtypeStruct((t, (N_HEADS // 2) * PAIR_W), BF16),
        compiler_params=_params(("parallel",)),
        name="q_prep",
    )(qa, g_a, wn, wp, g_n_t, g_p_t, cos_t, sin_t,
      _block_diag_ones(256, NOPE_DIM), _block_diag_ones(256, ROPE_DIM))


def _ffn_body(x_ref, g_ref, wg_ref, wu_ref, wd_ref, o_ref, h_ref):
    x = x_ref[...]
    inv = lax.rsqrt(jnp.mean(x * x, axis=-1, keepdims=True) + EPS)
    xn = (x * inv * g_ref[...]).astype(BF16)
    _swiglu_hidden(xn, wg_ref, wu_ref, h_ref, ())
    o_ref[...] = x + _dot(h_ref[...], wd_ref[...])


def dense_ffn(x, g, wg, wu, wd, layer, tm):
    t, d = x.shape
    f = wg.shape[2]
    once = pl.Buffered(1)
    return pl.pallas_call(
        _ffn_body,
        grid=(t // tm,),
        in_specs=[pl.BlockSpec((tm, d), lambda i: (i, 0)),
                  pl.BlockSpec((1, d), lambda i: (0, 0)),
                  pl.BlockSpec((None, d, f), lambda i: (layer, 0, 0), pipeline_mode=once),
                  pl.BlockSpec((None, d, f), lambda i: (layer, 0, 0), pipeline_mode=once),
                  pl.BlockSpec((None, f, d), lambda i: (layer, 0, 0), pipeline_mode=once)],
        out_specs=pl.BlockSpec((tm, d), lambda i: (i, 0)),
        out_shape=jax.ShapeDtypeStruct((t, d), F32),
        scratch_shapes=[pltpu.VMEM((tm, f), BF16)],
        compiler_params=_params(("parallel",)),
        name="dense_ffn",
    )(x, g.reshape(1, d), wg, wu, wd)


def _router_body(x_ref, g_ref, wr_ref, b_ref, upper_ref,
                 xn_ref, idx_ref, gate_ref, rank_ref, cnt_ref, base_ref):
    i = pl.program_id(0)

    @pl.when(i == 0)
    def _():
        base_ref[...] = jnp.zeros_like(base_ref)

    x = x_ref[...]
    xn = x * lax.rsqrt(jnp.mean(x * x, axis=-1, keepdims=True) + EPS) * g_ref[...]
    xn_ref[...] = xn
    tm = x.shape[0]

    x_hi, x_lo = _split_bf16(xn)
    w_hi, w_lo = _split_bf16(wr_ref[...])
    logits = _dot_nt(w_hi, x_hi) + _dot_nt(w_hi, x_lo) + _dot_nt(w_lo, x_hi) + b_ref[...]

    e_iota = lax.broadcasted_iota(I32, (N_EXPERTS, tm), 0)
    m1 = jnp.max(logits, axis=0, keepdims=True)
    i1 = jnp.min(jnp.where(logits == m1, e_iota, N_EXPERTS), axis=0, keepdims=True)
    sel1 = e_iota == i1
    rest = jnp.where(sel1, -jnp.inf, logits)
    m2 = jnp.max(rest, axis=0, keepdims=True)
    i2 = jnp.min(jnp.where(rest == m2, e_iota, N_EXPERTS), axis=0, keepdims=True)
    sel2 = e_iota == i2
    e2 = jnp.exp(m2 - m1)
    g1 = 1.0 / (1.0 + e2)
    idx_ref[...] = jnp.concatenate([i1, i2], axis=0)
    gate_ref[...] = jnp.concatenate([g1, e2 * g1], axis=0)

    member = jnp.logical_or(sel1, sel2)
    prefix = _dot(member.astype(BF16), upper_ref[...])
    rank = prefix + base_ref[...]
    r1 = jnp.sum(jnp.where(sel1, rank, 0.0), axis=0, keepdims=True)
    r2 = jnp.sum(jnp.where(sel2, rank, 0.0), axis=0, keepdims=True)
    rank_ref[...] = jnp.concatenate([r1, r2], axis=0).astype(I32)
    base_ref[...] += jnp.sum(member.astype(F32), axis=1, keepdims=True)
    cnt_ref[...] = jnp.broadcast_to(base_ref[...], cnt_ref.shape).astype(I32)


def moe_router(x, g, w_router, b_router, tm):
    t, d = x.shape
    a = lax.broadcasted_iota(I32, (tm, tm), 0)
    b = lax.broadcasted_iota(I32, (tm, tm), 1)
    upper = (a < b).astype(BF16)
    return pl.pallas_call(
        _router_body,
        grid=(t // tm,),
        in_specs=[pl.BlockSpec((tm, d), lambda i: (i, 0)),
                  _full((1, d)), _full((N_EXPERTS, d)), _full((N_EXPERTS, 1)), _full((tm, tm))],
        out_specs=[pl.BlockSpec((tm, d), lambda i: (i, 0)),
                   pl.BlockSpec((2, tm), lambda i: (0, i)),
                   pl.BlockSpec((2, tm), lambda i: (0, i)),
                   pl.BlockSpec((2, tm), lambda i: (0, i)),
                   _full((N_EXPERTS, LANES))],
        out_shape=[jax.ShapeDtypeStruct((t, d), F32),
                   jax.ShapeDtypeStruct((2, t), I32),
                   jax.ShapeDtypeStruct((2, t), F32),
                   jax.ShapeDtypeStruct((2, t), I32),
                   jax.ShapeDtypeStruct((N_EXPERTS, LANES), I32)],
        scratch_shapes=[pltpu.VMEM((N_EXPERTS, 1), F32)],
        compiler_params=_params(("arbitrary",)),
        name="moe_router",
    )(x, g.reshape(1, d), w_router.T, b_router.reshape(N_EXPERTS, 1), upper)


def _row_copy(src, src_row, dst, dst_row, sem):
    return pltpu.make_async_copy(src.at[pl.ds(src_row, 1)], dst.at[pl.ds(dst_row, 1)], sem)


def _dispatch_body(dest_ref, xn_ref, buf_in_hbm, buf_hbm, sem, *, tb, t_total):
    del buf_in_hbm
    base = pl.program_id(0) * tb

    def issue(t, c):
        for k in range(2):
            _row_copy(xn_ref, t, buf_hbm, dest_ref[k * t_total + base + t], sem).start()
        return c

    lax.fori_loop(0, tb, issue, 0, unroll=8)

    def drain(t, c):
        for k in range(2):
            _row_copy(xn_ref, 0, buf_hbm, 0, sem).wait()
        return c

    lax.fori_loop(0, tb, drain, 0, unroll=8)


def moe_dispatch(dest_flat, xn, n_rows, tb):
    t, d = xn.shape
    buf0 = jnp.zeros((n_rows, d), xn.dtype)
    grid_spec = pltpu.PrefetchScalarGridSpec(
        num_scalar_prefetch=1,
        grid=(t // tb,),
        in_specs=[pl.BlockSpec((tb, d), lambda i, dest: (i, 0)), pl.BlockSpec(memory_space=pl.ANY)],
        out_specs=pl.BlockSpec(memory_space=pl.ANY),
        scratch_shapes=[pltpu.SemaphoreType.DMA(())],
    )
    return pl.pallas_call(
        functools.partial(_dispatch_body, tb=tb, t_total=t),
        grid_spec=grid_spec,
        out_shape=jax.ShapeDtypeStruct((n_rows, d), xn.dtype),
        input_output_aliases={2: 0},
        compiler_params=_params(("arbitrary",)),
        name="moe_dispatch",
    )(dest_flat, xn, buf0)


def _swiglu_hidden(xb, wg_ref, wu_ref, h_ref, widx):
    f = h_ref.shape[1]
    for c0 in range(0, f, MXU_TILE):
        c1 = min(c0 + MXU_TILE, f)
        gate = _dot(xb, wg_ref[widx + (slice(None), slice(c0, c1))])
        up = _dot(xb, wu_ref[widx + (slice(None), slice(c0, c1))])
        h_ref[:, c0:c1] = (gate * jax.nn.sigmoid(gate) * up).astype(BF16)


def _experts_body(be_ref, nv_ref, x_ref, wg_ref, wu_ref, wd_ref, o_ref, h_ref):
    del be_ref
    valid = pl.program_id(0) < nv_ref[0]

    @pl.when(valid)
    def _():
        _swiglu_hidden(x_ref[...].astype(BF16), wg_ref, wu_ref, h_ref, (0,))
        o_ref[...] = _dot(h_ref[...], wd_ref[0])

    @pl.when(jnp.logical_not(valid))
    def _():
        o_ref[...] = jnp.zeros_like(o_ref)


def moe_experts(block_e, n_valid, buf, wg, wu, wd, layer, tm):
    n_rows, d = buf.shape
    f = wg.shape[3]
    once = pl.Buffered(1)
    grid_spec = pltpu.PrefetchScalarGridSpec(
        num_scalar_prefetch=2,
        grid=(n_rows // tm,),
        in_specs=[pl.BlockSpec((tm, d), lambda i, be, nv: (i, 0)),
                  pl.BlockSpec((None, 1, d, f), lambda i, be, nv: (layer, be[i], 0, 0), pipeline_mode=once),
                  pl.BlockSpec((None, 1, d, f), lambda i, be, nv: (layer, be[i], 0, 0), pipeline_mode=once),
                  pl.BlockSpec((None, 1, f, d), lambda i, be, nv: (layer, be[i], 0, 0), pipeline_mode=once)],
        out_specs=pl.BlockSpec((tm, d), lambda i, be, nv: (i, 0)),
        scratch_shapes=[pltpu.VMEM((tm, f), BF16)],
    )
    return pl.pallas_call(
        _experts_body,
        grid_spec=grid_spec,
        out_shape=jax.ShapeDtypeStruct((n_rows, d), F32),
        compiler_params=_params(("arbitrary",), vmem=VMEM_LIMIT_BIG),
        name="moe_experts",
    )(block_e, n_valid, buf, wg, wu, wd)


def _combine_body(dest_ref, x_ref, gate_ref, eo_hbm, o_ref, buf_ref, sem, *, tb, t_total):
    base = pl.program_id(0) * tb

    def issue(t, c):
        for k in range(2):
            _row_copy(eo_hbm, dest_ref[k * t_total + base + t], buf_ref.at[k], t, sem).start()
        return c

    lax.fori_loop(0, tb, issue, 0, unroll=8)

    def drain(t, c):
        for k in range(2):
            _row_copy(eo_hbm, 0, buf_ref.at[k], 0, sem).wait()
        return c

    lax.fori_loop(0, tb, drain, 0, unroll=8)
    gates = gate_ref[...]
    o_ref[...] = x_ref[...] + gates[:, 0:1] * buf_ref[0] + gates[:, 1:2] * buf_ref[1]


def moe_combine(dest_flat, x, gates_t, expert_out, tb):
    t, d = x.shape
    grid_spec = pltpu.PrefetchScalarGridSpec(
        num_scalar_prefetch=1,
        grid=(t // tb,),
        in_specs=[pl.BlockSpec((tb, d), lambda i, dest: (i, 0)),
                  pl.BlockSpec((tb, 2), lambda i, dest: (i, 0)),
                  pl.BlockSpec(memory_space=pl.ANY)],
        out_specs=pl.BlockSpec((tb, d), lambda i, dest: (i, 0)),
        scratch_shapes=[pltpu.VMEM((2, tb, d), F32), pltpu.SemaphoreType.DMA(())],
    )
    return pl.pallas_call(
        functools.partial(_combine_body, tb=tb, t_total=t),
        grid_spec=grid_spec,
        out_shape=jax.ShapeDtypeStruct((t, d), F32),
        compiler_params=_params(("arbitrary",)),
        name="moe_combine",
    )(dest_flat, x, gates_t, expert_out)


def moe_layer(x, g, w_router, b_router, wg, wu, wd, layer, tiles):
    t, d = x.shape
    tm = tiles["moe_tm"]
    xn, idx, gates, rank, cnt = moe_router(x, g, w_router, b_router, tiles["router_tm"])
    counts = cnt[:, 0]
    padded = ((counts + tm - 1) // tm) * tm
    pends = jnp.cumsum(padded)
    pstarts = pends - padded
    group_start = sum(jnp.where(idx == e, pstarts[e], 0) for e in range(N_EXPERTS))
    dest_flat = (group_start + rank).reshape(-1)
    n_rows = 2 * t + N_EXPERTS * tm
    n_blocks = n_rows // tm
    block_row = jnp.arange(n_blocks, dtype=I32) * tm
    block_e = jnp.minimum(sum((block_row >= pends[e]).astype(I32) for e in range(N_EXPERTS)),
                          N_EXPERTS - 1)
    n_valid = (pends[-1:] // tm).astype(I32)
    buf = moe_dispatch(dest_flat, xn, n_rows, tiles["moe_tb"])
    eo = moe_experts(block_e, n_valid, buf, wg, wu, wd, layer, tm)
    return moe_combine(dest_flat, x, gates.T, eo, tiles["moe_tb"])


def _tiles(t, s):
    def fit(n, want):
        while n % want:
            want //= 2
        return want
    return dict(
        proj_tm=fit(t, 1024), attn_tq=fit(s, 256), mla_tq=fit(s, 512), prep_tm=fit(s, 512),
        ffn_tm=fit(t, 512), router_tm=fit(t, 512), moe_tm=fit(t, 512), moe_tb=fit(t, 256),
    )


def _rope_tables(seq):
    pos = jnp.arange(seq, dtype=F32)
    inv_freq = ROPE_THETA ** (-jnp.arange(0, ROPE_DIM, 2, dtype=F32) / ROPE_DIM)
    ang = pos[:, None] * inv_freq[None, :]
    ang = jnp.concatenate([ang, ang], axis=-1)
    reps = LANES // ROPE_DIM
    return jnp.tile(jnp.cos(ang), (1, reps)), jnp.tile(jnp.sin(ang), (1, reps))


def kernel(x, g_mix, g_ffn, sb_w_qkv, sb_w_o, kv_g_src, kv_w_a, kv_g_a, kv_w_b, kv_g_k_nope, kv_g_k_pe, mla_w_q_a, mla_g_q_a, mla_w_q_b, mla_g_q_nope, mla_g_q_pe, mla_w_o, ffn_w_gate, ffn_w_up, ffn_w_down, moe_w_router, moe_b_router, moe_w_gate, moe_w_up, moe_w_down):
    b, s, d = x.shape
    t = b * s
    depth = g_mix.shape[0]
    n_a = sb_w_qkv.shape[0]
    tiles = _tiles(t, s)
    cos_t, sin_t = _rope_tables(s)
    h = x.reshape(t, d)
    hw = N_HEADS * SB_HEAD_DIM
    kcat = vcat = None
    ffn_w = [w.astype(BF16) for w in (ffn_w_gate, ffn_w_up, ffn_w_down)]
    moe_w = [w.astype(BF16) for w in (moe_w_gate, moe_w_up, moe_w_down)]

    for i in range(depth):
        if i < n_a:
            col_scale = jnp.concatenate([jnp.full((hw,), LOG2E / math.sqrt(SB_HEAD_DIM), F32),
                                         jnp.ones((2 * hw,), F32)])
            w_qkv = (sb_w_qkv[i] * col_scale).astype(BF16)
            qkv = rms_matmul(h, g_mix[i], w_qkv, BF16, tiles["ffn_tm"], w_qkv.shape[1])
            att = sb_attention(qkv.reshape(b, s, 3 * hw), tiles["attn_tq"])
            h = matmul_residual(att.reshape(t, hw), sb_w_o[i].astype(BF16), h, tiles["proj_tm"])
        else:
            j = i - n_a
            if kcat is None:
                pad = jnp.zeros((d, LANES - ROPE_DIM), F32)
                w_a = jnp.concatenate([kv_w_a, pad], axis=1).astype(BF16)
                kva = rms_matmul(h, kv_g_src, w_a, F32, tiles["proj_tm"], w_a.shape[1])
                w_b = kv_w_b.reshape(KV_RANK, N_HEADS, NOPE_DIM + V_DIM)
                wn = w_b[:, :, :NOPE_DIM].reshape(KV_RANK, N_HEADS * NOPE_DIM).astype(BF16)
                wv = w_b[:, :, NOPE_DIM:].reshape(KV_RANK, N_HEADS * V_DIM).astype(BF16)
                g_pe_pad = jnp.concatenate([kv_g_k_pe, jnp.zeros((LANES - ROPE_DIM,), F32)]).reshape(1, LANES)
                kcat, vcat = kv_prep(kva, kv_g_a.reshape(1, KV_RANK), g_pe_pad, cos_t, sin_t, wn, wv,
                                     jnp.tile(kv_g_k_nope, N_HEADS).reshape(1, -1), tiles["prep_tm"], s)
                kcat = kcat.reshape(b, s, -1)
                vcat = vcat.reshape(b, s, -1)
            qa = rms_matmul(h, g_mix[i], mla_w_q_a[j].astype(BF16), F32, tiles["proj_tm"], mla_w_q_a.shape[2])
            w_qb = mla_w_q_b[j].reshape(-1, N_HEADS, NOPE_DIM + ROPE_DIM)
            wqn = w_qb[:, :, :NOPE_DIM].reshape(-1, N_HEADS * NOPE_DIM).astype(BF16)
            wqp = w_qb[:, :, NOPE_DIM:].reshape(-1, N_HEADS * ROPE_DIM).astype(BF16)
            scale = LOG2E / math.sqrt(NOPE_DIM + ROPE_DIM)
            qcat = q_prep(qa, mla_g_q_a[j].reshape(1, -1), wqn, wqp,
                          jnp.tile(mla_g_q_nope[j], N_HEADS).reshape(1, -1),
                          jnp.tile(mla_g_q_pe[j], N_HEADS).reshape(1, -1),
                          cos_t, sin_t, tiles["prep_tm"], s, scale)
            att = mla_attention(qcat.reshape(b, s, -1), kcat, vcat, tiles["mla_tq"])
            h = matmul_residual(att.reshape(t, -1), mla_w_o[j].astype(BF16), h, tiles["proj_tm"])

        m = i // 2
        if i % 2 == 0:
            h = dense_ffn(h, g_ffn[i], *ffn_w, m, tiles["ffn_tm"])
        else:
            h = moe_layer(h, g_ffn[i], moe_w_router[m], moe_b_router[m], *moe_w, m, tiles)
    return h.reshape(b, s, d)
```

```python
import functools
import math

import jax
import jax.numpy as jnp
from jax import lax
from jax.experimental import pallas as pl
from jax.experimental.pallas import tpu as pltpu

F32 = jnp.float32
BF16 = jnp.bfloat16
I32 = jnp.int32

EPS = 1e-6
N_HEADS = 16
SB_HEAD_DIM = 64
NOPE_DIM = 64
ROPE_DIM = 32
V_DIM = 64
KV_RANK = 256
ROPE_THETA = 10000.0
N_EXPERTS = 8
LANES = 128
PAIR_W = 2 * NOPE_DIM + LANES
ONES_LANE = 2 * NOPE_DIM + 2 * ROPE_DIM
MLA_STATIC_SHIFT_MAX = 50.0
MLA_GROUP = 4
LOG2E = 1.4426950408889634
SB_SKIP_EXP = 110.0
SB_SUBS = 2
MXU_TILE = 256
VMEM_LIMIT = 52 * 1024 * 1024
VMEM_LIMIT_BIG = 58 * 1024 * 1024


def _params(sem, vmem=VMEM_LIMIT):
    return pltpu.CompilerParams(dimension_semantics=sem, vmem_limit_bytes=vmem)


def _dot(a, b):
    return jnp.dot(a, b, preferred_element_type=F32)


def _dot_nt(a, b):
    return lax.dot_general(a, b, (((1,), (1,)), ((), ())), preferred_element_type=F32)


def _split_bf16(x):
    hi = x.astype(BF16)
    lo = (x - hi.astype(F32)).astype(BF16)
    return hi, lo


def _dot_hilo(x, w):
    hi, lo = _split_bf16(x)
    return _dot(hi, w) + _dot(lo, w)


def _rms_matmul_body(x_ref, g_ref, w_ref, o_ref, xn_ref):
    @pl.when(pl.program_id(1) == 0)
    def _():
        x = x_ref[...]
        inv = lax.rsqrt(jnp.mean(x * x, axis=-1, keepdims=True) + EPS)
        xn_ref[...] = (x * inv * g_ref[...]).astype(BF16)

    o_ref[...] = _dot(xn_ref[...], w_ref[...]).astype(o_ref.dtype)


def rms_matmul(x, g, w, out_dtype, tm, tn):
    t, d = x.shape
    n = w.shape[1]
    w_mode = pl.Buffered(1) if tn == n else None
    return pl.pallas_call(
        _rms_matmul_body,
        grid=(t // tm, n // tn),
        in_specs=[pl.BlockSpec((tm, d), lambda i, j: (i, 0)),
                  pl.BlockSpec((1, d), lambda i, j: (0, 0)),
                  pl.BlockSpec((d, tn), lambda i, j: (0, j), pipeline_mode=w_mode)],
        out_specs=pl.BlockSpec((tm, tn), lambda i, j: (i, j)),
        out_shape=jax.ShapeDtypeStruct((t, n), out_dtype),
        scratch_shapes=[pltpu.VMEM((tm, d), BF16)],
        compiler_params=_params(("parallel", "arbitrary")),
        name="rms_matmul",
    )(x, g.reshape(1, d), w)


def _matmul_res_body(a_ref, w_ref, r_ref, o_ref):
    o_ref[...] = r_ref[...] + _dot(a_ref[...], w_ref[...])


def matmul_residual(a, w, res, tm):
    t, k = a.shape
    n = w.shape[1]
    return pl.pallas_call(
        _matmul_res_body,
        grid=(t // tm,),
        in_specs=[pl.BlockSpec((tm, k), lambda i: (i, 0)),
                  pl.BlockSpec((k, n), lambda i: (0, 0)),
                  pl.BlockSpec((tm, n), lambda i: (i, 0))],
        out_specs=pl.BlockSpec((tm, n), lambda i: (i, 0)),
        out_shape=jax.ShapeDtypeStruct((t, n), F32),
        compiler_params=_params(("parallel",)),
        name="matmul_residual",
    )(a, w, res)


def _softplus2(z):
    return jnp.maximum(jnp.log2(1.0 + jnp.exp2(jnp.minimum(z, 64.0))), z)


def _key_norm_max(k_ref, head_masks, kmax_ref):
    k = k_ref[0].astype(F32)
    k2 = k * k
    for hh, mask in enumerate(head_masks):
        n2 = jnp.sum(jnp.where(mask, k2, 0.0), axis=-1, keepdims=True)
        kmax_ref[hh] = jnp.max(n2, axis=0, keepdims=True)


def _logit_bound(qm, kmax):
    qf = qm.astype(F32)
    return jnp.sqrt(jnp.sum(qf * qf, axis=-1, keepdims=True) * kmax) * 1.01 + 1e-2


def _sb_attn_body(q_ref, k_ref, v_ref, tri_ref, o_ref, kmax_ref, qs_ref, zb_ref, carry_ref, acc_ref,
                  *, tq):
    step = pl.program_id(2)
    lane = lax.broadcasted_iota(I32, (1, LANES), 1)
    head_masks = [(lane // SB_HEAD_DIM) == hh for hh in range(2)]

    @pl.when(step == 0)
    def _():
        _key_norm_max(k_ref, head_masks, kmax_ref)

    for sub in range(SB_SUBS):
        q2 = q_ref[0, sub * tq:(sub + 1) * tq, :]
        for hh in range(2):
            qm = jnp.where(head_masks[hh], q2, jnp.zeros_like(q2))
            qs_ref[2 * sub + hh] = qm
            zb_ref[2 * sub + hh] = _logit_bound(qm, kmax_ref[hh])

    row = lax.broadcasted_iota(I32, (tq, tq), 0)
    col = lax.broadcasted_iota(I32, (tq, tq), 1)
    causal = col < row

    def chain(c, start, masked):
        z = _dot_nt(qs_ref[c], k_ref[0, pl.ds(start, tq), :])
        sp = _softplus2(z)
        if masked:
            sp = jnp.where(causal, sp, 0.0)
        csum = _dot(sp.astype(BF16), tri_ref[...])
        e = jnp.exp2(z - csum)
        if masked:
            e = jnp.where(causal, e, 0.0)
        return _dot(e.astype(BF16), v_ref[0, pl.ds(start, tq), :]), csum[:, 0:1]

    for sub in range(SB_SUBS):
        qi = SB_SUBS * step + sub
        has_prev = qi >= 1
        prev_start = pl.multiple_of(jnp.maximum(qi - 1, 0) * tq, tq)
        for hh in range(2):
            c = 2 * sub + hh
            pv_d, tot_d = chain(c, pl.multiple_of(qi * tq, tq), True)
            pv_p, tot_p = chain(c, prev_start, False)
            acc_ref[c] = pv_d + jnp.where(has_prev, jnp.exp2(-tot_d), 0.0) * pv_p
            carry_ref[c] = tot_d + jnp.where(has_prev, tot_p, 0.0)

    for sub in range(SB_SUBS):
        qi = SB_SUBS * step + sub
        chains = slice(2 * sub, 2 * sub + 2)

        def cond(jj):
            live = jnp.max(zb_ref[chains] - carry_ref[chains]) > -SB_SKIP_EXP * LOG2E
            return jnp.logical_and(jj <= qi, live)

        def body(jj):
            start = pl.multiple_of((qi - jj) * tq, tq)
            for c in range(2 * sub, 2 * sub + 2):
                pv, tot = chain(c, start, False)
                carry = carry_ref[c]
                acc_ref[c] += jnp.exp2(-carry) * pv
                carry_ref[c] = carry + tot
            return jj + 1

        lax.while_loop(cond, body, jnp.int32(2))
        o_ref[0, sub * tq:(sub + 1) * tq, :] = jnp.where(
            head_masks[0], acc_ref[2 * sub], acc_ref[2 * sub + 1]).astype(o_ref.dtype)


def sb_attention(qkv, tq):
    b, s, _ = qkv.shape
    n_pairs = N_HEADS // 2
    rows = SB_SUBS * tq
    n_chains = 2 * SB_SUBS
    ji = lax.broadcasted_iota(I32, (tq, tq), 0)
    si = lax.broadcasted_iota(I32, (tq, tq), 1)
    tri = (ji >= si).astype(BF16)
    return pl.pallas_call(
        functools.partial(_sb_attn_body, tq=tq),
        grid=(b, n_pairs, s // rows),
        in_specs=[pl.BlockSpec((1, rows, LANES), lambda bi, p, i: (bi, i, p)),
                  pl.BlockSpec((1, s, LANES), lambda bi, p, i: (bi, 0, n_pairs + p)),
                  pl.BlockSpec((1, s, LANES), lambda bi, p, i: (bi, 0, 2 * n_pairs + p)),
                  pl.BlockSpec((tq, tq), lambda bi, p, i: (0, 0))],
        out_specs=pl.BlockSpec((1, rows, LANES), lambda bi, p, i: (bi, i, p)),
        out_shape=jax.ShapeDtypeStruct((b, s, N_HEADS * SB_HEAD_DIM), BF16),
        scratch_shapes=[pltpu.VMEM((2, 1, 1), F32),
                        pltpu.VMEM((n_chains, tq, LANES), BF16),
                        pltpu.VMEM((n_chains, tq, 1), F32),
                        pltpu.VMEM((n_chains, tq, 1), F32),
                        pltpu.VMEM((n_chains, tq, LANES), F32)],
        compiler_params=_params(("parallel", "parallel", "arbitrary")),
        name="sb_attention",
    )(qkv, qkv, qkv, tri)


def _mla_attn_body(q_ref, k_ref, v_ref, o_ref, kmax_ref, qs_ref, m_ref, acc_ref, *, tq):
    qi = pl.program_id(2)
    lane_q = lax.broadcasted_iota(I32, (1, PAIR_W), 1)
    lane_o = lax.broadcasted_iota(I32, (1, LANES), 1)
    head_masks = [jnp.logical_or((lane_q // NOPE_DIM) == hh,
                                 jnp.logical_and(lane_q >= 2 * NOPE_DIM,
                                                 ((lane_q - 2 * NOPE_DIM) // ROPE_DIM) == hh))
                  for hh in range(2)]

    @pl.when(qi == 0)
    def _():
        _key_norm_max(k_ref, head_masks, kmax_ref)

    q2 = q_ref[0]
    shift_max = jnp.float32(0.0)
    for hh in range(2):
        qm = jnp.where(head_masks[hh], q2, jnp.zeros_like(q2))
        bound = _logit_bound(qm, kmax_ref[hh])
        qs_ref[hh] = jnp.where(lane_q == ONES_LANE, (-bound).astype(BF16), qm)
        shift_max = jnp.maximum(shift_max, jnp.max(bound))
        acc_ref[hh] = jnp.zeros((tq, 2 * LANES), F32)

    ones = jnp.ones((tq, LANES), BF16)
    causal = (lax.broadcasted_iota(I32, (tq, tq), 1) <= lax.broadcasted_iota(I32, (tq, tq), 0))

    def scores(hh, j, masked):
        sc = _dot_nt(qs_ref[hh], k_ref[0, pl.ds(pl.multiple_of(j * tq, tq), tq), :])
        return jnp.where(causal, sc, -jnp.inf) if masked else sc

    def values(j):
        return jnp.concatenate([v_ref[0, pl.ds(pl.multiple_of(j * tq, tq), tq), :], ones], axis=-1)

    def static_blocks(blocks):
        for hh in range(2):
            acc_ref[hh] += sum(_dot(jnp.exp2(scores(hh, j, masked)).astype(BF16), values(j))
                               for j, masked in blocks)

    def online_block(j, masked):
        for hh in range(2):
            sc = scores(hh, j, masked)
            m_prev = m_ref[hh]
            m_new = jnp.maximum(m_prev, jnp.max(sc, axis=-1, keepdims=True))
            p = jnp.exp2(sc - m_new)
            acc_ref[hh] = jnp.exp2(m_prev - m_new) * acc_ref[hh] + _dot(p.astype(BF16), values(j))
            m_ref[hh] = m_new

    @pl.when(shift_max <= MLA_STATIC_SHIFT_MAX)
    def _():
        def group(j, c):
            static_blocks([(MLA_GROUP * j + u, False) for u in range(MLA_GROUP)])
            return c
        lax.fori_loop(0, qi // MLA_GROUP, group, 0)

        for rest in range(MLA_GROUP):
            @pl.when(qi % MLA_GROUP == rest)
            def _():
                static_blocks([(qi - rest + u, False) for u in range(rest)] + [(qi, True)])

    @pl.when(shift_max > MLA_STATIC_SHIFT_MAX)
    def _():
        m_ref[...] = jnp.full_like(m_ref, -jnp.inf)

        def body(j, c):
            online_block(j, False)
            return c
        lax.fori_loop(0, qi, body, 0)
        online_block(qi, True)

    outs = [acc_ref[hh][:, :LANES] / acc_ref[hh][:, LANES:] for hh in range(2)]
    o_ref[0] = jnp.where(lane_o < V_DIM, outs[0], outs[1]).astype(o_ref.dtype)


def mla_attention(qcat, kcat, v, tq):
    b, s, _ = qcat.shape
    n_pairs = N_HEADS // 2
    return pl.pallas_call(
        functools.partial(_mla_attn_body, tq=tq),
        grid=(b, n_pairs, s // tq),
        in_specs=[pl.BlockSpec((1, tq, PAIR_W), lambda bi, p, i: (bi, i, p)),
                  pl.BlockSpec((1, s, PAIR_W), lambda bi, p, i: (bi, 0, p)),
                  pl.BlockSpec((1, s, LANES), lambda bi, p, i: (bi, 0, p))],
        out_specs=pl.BlockSpec((1, tq, LANES), lambda bi, p, i: (bi, i, p)),
        out_shape=jax.ShapeDtypeStruct((b, s, N_HEADS * V_DIM), BF16),
        scratch_shapes=[pltpu.VMEM((2, 1, 1), F32),
                        pltpu.VMEM((2, tq, PAIR_W), BF16),
                        pltpu.VMEM((2, tq, 1), F32),
                        pltpu.VMEM((2, tq, 2 * LANES), F32)],
        compiler_params=_params(("parallel", "parallel", "arbitrary")),
        name="mla_attention",
    )(qcat, kcat, v)


def _group_rms(x, gmat, group, g):
    chunks = []
    for c in range(x.shape[1] // 256):
        xc = x[:, c * 256:(c + 1) * 256]
        ss = _dot_hilo(xc * xc, gmat)
        chunks.append(xc * lax.rsqrt(ss * (1.0 / group) + EPS))
    y = chunks[0] if len(chunks) == 1 else jnp.concatenate(chunks, axis=-1)
    return y * g


def _rope128(x, cos, sin):
    lane = lax.broadcasted_iota(I32, (1, LANES), 1)
    first_half = (lane % ROPE_DIM) < (ROPE_DIM // 2)
    rot = jnp.where(first_half,
                    -pltpu.roll(x, LANES - ROPE_DIM // 2, 1),
                    pltpu.roll(x, ROPE_DIM // 2, 1))
    return x * cos + rot * sin


def _kv_prep_body(kva_ref, g_a_ref, g_pe_ref, cos_ref, sin_ref, wn_ref, wv_ref, g_n_ref,
                  g64_ref, k_ref, v_ref):
    lane = lax.broadcasted_iota(I32, (1, LANES), 1)
    kva = kva_ref[...]
    c = kva[:, :KV_RANK]
    cn = (c * lax.rsqrt(jnp.mean(c * c, axis=-1, keepdims=True) + EPS) * g_a_ref[...]).astype(BF16)
    pe = kva[:, KV_RANK:KV_RANK + LANES]
    ms = jnp.sum(pe * pe, axis=-1, keepdims=True) * (1.0 / ROPE_DIM)
    pen = pe * lax.rsqrt(ms + EPS) * g_pe_ref[...]
    kpe = _rope128(pen, cos_ref[...], sin_ref[...])
    kpe = jnp.where(lane < ROPE_DIM, kpe, 0.0)
    kpe2 = kpe + pltpu.roll(kpe, ROPE_DIM, 1)
    kpe2 = jnp.where(lane == ONES_LANE - 2 * NOPE_DIM, 1.0, kpe2).astype(BF16)
    kn = _group_rms(_dot(cn, wn_ref[...]), g64_ref[...], NOPE_DIM, g_n_ref[...]).astype(BF16)
    v_ref[...] = _dot(cn, wv_ref[...]).astype(BF16)
    for p in range(N_HEADS // 2):
        k_ref[:, p * PAIR_W:p * PAIR_W + LANES] = kn[:, p * LANES:(p + 1) * LANES]
        k_ref[:, p * PAIR_W + LANES:(p + 1) * PAIR_W] = kpe2


def _q_prep_body(qa_ref, g_a_ref, wn_ref, wp_ref, g_n_ref, g_p_ref, cos_ref, sin_ref,
                 g64_ref, g32_ref, q_ref, *, scale):
    lane = lax.broadcasted_iota(I32, (1, LANES), 1)
    qa = qa_ref[...]
    qan = (qa * lax.rsqrt(jnp.mean(qa * qa, axis=-1, keepdims=True) + EPS) * g_a_ref[...]).astype(BF16)
    qn = _group_rms(_dot(qan, wn_ref[...]), g64_ref[...], NOPE_DIM, g_n_ref[...]) * scale
    qp = _group_rms(_dot(qan, wp_ref[...]), g32_ref[...], ROPE_DIM, g_p_ref[...])
    cos = cos_ref[...]
    sin = sin_ref[...]
    for c in range(N_HEADS * ROPE_DIM // LANES):
        pe4 = _rope128(qp[:, c * LANES:(c + 1) * LANES], cos, sin) * scale
        for half in range(2):
            p = 2 * c + half
            x = pe4 if half == 0 else pltpu.roll(pe4, 2 * ROPE_DIM, 1)
            q_ref[:, p * PAIR_W:p * PAIR_W + LANES] = qn[:, p * LANES:(p + 1) * LANES].astype(BF16)
            q_ref[:, p * PAIR_W + LANES:(p + 1) * PAIR_W] = jnp.where(lane < 2 * ROPE_DIM, x, 0.0).astype(BF16)


def _block_diag_ones(n, group):
    a = lax.broadcasted_iota(I32, (n, n), 0) // group
    b = lax.broadcasted_iota(I32, (n, n), 1) // group
    return (a == b).astype(BF16)


def _full(shape):
    return pl.BlockSpec(shape, lambda i: (0,) * len(shape))


def kv_prep(kva, g_a, g_pe_pad, cos_t, sin_t, wn, wv, g_n_t, tm, seq):
    t = kva.shape[0]
    nseq = seq // tm
    return pl.pallas_call(
        _kv_prep_body,
        grid=(t // tm,),
        in_specs=[pl.BlockSpec((tm, kva.shape[1]), lambda i: (i, 0)),
                  _full((1, KV_RANK)), _full((1, LANES)),
                  pl.BlockSpec((tm, LANES), lambda i: (i % nseq, 0)),
                  pl.BlockSpec((tm, LANES), lambda i: (i % nseq, 0)),
                  _full(wn.shape), _full(wv.shape), _full((1, wn.shape[1])),
                  _full((256, 256))],
        out_specs=[pl.BlockSpec((tm, (N_HEADS // 2) * PAIR_W), lambda i: (i, 0)),
                   pl.BlockSpec((tm, N_HEADS * V_DIM), lambda i: (i, 0))],
        out_shape=[jax.ShapeDtypeStruct((t, (N_HEADS // 2) * PAIR_W), BF16),
                   jax.ShapeDtypeStruct((t, N_HEADS * V_DIM), BF16)],
        compiler_params=_params(("parallel",)),
        name="kv_prep",
    )(kva, g_a, g_pe_pad, cos_t, sin_t, wn, wv, g_n_t, _block_diag_ones(256, NOPE_DIM))


def q_prep(qa, g_a, wn, wp, g_n_t, g_p_t, cos_t, sin_t, tm, seq, scale):
    t = qa.shape[0]
    nseq = seq // tm
    return pl.pallas_call(
        functools.partial(_q_prep_body, scale=scale),
        grid=(t // tm,),
        in_specs=[pl.BlockSpec((tm, qa.shape[1]), lambda i: (i, 0)),
                  _full((1, qa.shape[1])), _full(wn.shape), _full(wp.shape),
                  _full((1, wn.shape[1])), _full((1, wp.shape[1])),
                  pl.BlockSpec((tm, LANES), lambda i: (i % nseq, 0)),
                  pl.BlockSpec((tm, LANES), lambda i: (i % nseq, 0)),
                  _full((256, 256)), _full((256, 256))],
        out_specs=pl.BlockSpec((tm, (N_HEADS // 2) * PAIR_W), lambda i: (i, 0)),
        out_shape=jax.ShapeDtypeStruct((t, (N_HEADS // 2) * PAIR_W), BF16),
        compiler_params=_params(("parallel",)),
        name="q_prep",
    )(qa, g_a, wn, wp, g_n_t, g_p_t, cos_t, sin_t,
      _block_diag_ones(256, NOPE_DIM), _block_diag_ones(256, ROPE_DIM))


def _ffn_body(x_ref, g_ref, wg_ref, wu_ref, wd_ref, o_ref, h_ref):
    x = x_ref[...]
    inv = lax.rsqrt(jnp.mean(x * x, axis=-1, keepdims=True) + EPS)
    xn = (x * inv * g_ref[...]).astype(BF16)
    _swiglu_hidden(xn, wg_ref, wu_ref, h_ref, ())
    o_ref[...] = x + _dot(h_ref[...], wd_ref[...])


def dense_ffn(x, g, wg, wu, wd, layer, tm):
    t, d = x.shape
    f = wg.shape[2]
    once = pl.Buffered(1)
    return pl.pallas_call(
        _ffn_body,
        grid=(t // tm,),
        in_specs=[pl.BlockSpec((tm, d), lambda i: (i, 0)),
                  pl.BlockSpec((1, d), lambda i: (0, 0)),
                  pl.BlockSpec((None, d, f), lambda i: (layer, 0, 0), pipeline_mode=once),
                  pl.BlockSpec((None, d, f), lambda i: (layer, 0, 0), pipeline_mode=once),
                  pl.BlockSpec((None, f, d), lambda i: (layer, 0, 0), pipeline_mode=once)],
        out_specs=pl.BlockSpec((tm, d), lambda i: (i, 0)),
        out_shape=jax.ShapeDtypeStruct((t, d), F32),
        scratch_shapes=[pltpu.VMEM((tm, f), BF16)],
        compiler_params=_params(("parallel",)),
        name="dense_ffn",
    )(x, g.reshape(1, d), wg, wu, wd)


def _router_body(x_ref, g_ref, wr_ref, b_ref, upper_ref,
                 xn_ref, idx_ref, gate_ref, rank_ref, cnt_ref, base_ref):
    i = pl.program_id(0)

    @pl.when(i == 0)
    def _():
        base_ref[...] = jnp.zeros_like(base_ref)

    x = x_ref[...]
    xn = x * lax.rsqrt(jnp.mean(x * x, axis=-1, keepdims=True) + EPS) * g_ref[...]
    xn_ref[...] = xn
    tm = x.shape[0]

    x_hi, x_lo = _split_bf16(xn)
    w_hi, w_lo = _split_bf16(wr_ref[...])
    logits = _dot_nt(w_hi, x_hi) + _dot_nt(w_hi, x_lo) + _dot_nt(w_lo, x_hi) + b_ref[...]

    e_iota = lax.broadcasted_iota(I32, (N_EXPERTS, tm), 0)
    m1 = jnp.max(logits, axis=0, keepdims=True)
    i1 = jnp.min(jnp.where(logits == m1, e_iota, N_EXPERTS), axis=0, keepdims=True)
    sel1 = e_iota == i1
    rest = jnp.where(sel1, -jnp.inf, logits)
    m2 = jnp.max(rest, axis=0, keepdims=True)
    i2 = jnp.min(jnp.where(rest == m2, e_iota, N_EXPERTS), axis=0, keepdims=True)
    sel2 = e_iota == i2
    e2 = jnp.exp(m2 - m1)
    g1 = 1.0 / (1.0 + e2)
    idx_ref[...] = jnp.concatenate([i1, i2], axis=0)
    gate_ref[...] = jnp.concatenate([g1, e2 * g1], axis=0)

    member = jnp.logical_or(sel1, sel2)
    prefix = _dot(member.astype(BF16), upper_ref[...])
    rank = prefix + base_ref[...]
    r1 = jnp.sum(jnp.where(sel1, rank, 0.0), axis=0, keepdims=True)
    r2 = jnp.sum(jnp.where(sel2, rank, 0.0), axis=0, keepdims=True)
    rank_ref[...] = jnp.concatenate([r1, r2], axis=0).astype(I32)
    base_ref[...] += jnp.sum(member.astype(F32), axis=1, keepdims=True)
    cnt_ref[...] = jnp.broadcast_to(base_ref[...], cnt_ref.shape).astype(I32)


def moe_router(x, g, w_router, b_router, tm):
    t, d = x.shape
    a = lax.broadcasted_iota(I32, (tm, tm), 0)
    b = lax.broadcasted_iota(I32, (tm, tm), 1)
    upper = (a < b).astype(BF16)
    return pl.pallas_call(
        _router_body,
        grid=(t // tm,),
        in_specs=[pl.BlockSpec((tm, d), lambda i: (i, 0)),
                  _full((1, d)), _full((N_EXPERTS, d)), _full((N_EXPERTS, 1)), _full((tm, tm))],
        out_specs=[pl.BlockSpec((tm, d), lambda i: (i, 0)),
                   pl.BlockSpec((2, tm), lambda i: (0, i)),
                   pl.BlockSpec((2, tm), lambda i: (0, i)),
                   pl.BlockSpec((2, tm), lambda i: (0, i)),
                   _full((N_EXPERTS, LANES))],
        out_shape=[jax.ShapeDtypeStruct((t, d), F32),
                   jax.ShapeDtypeStruct((2, t), I32),
                   jax.ShapeDtypeStruct((2, t), F32),
                   jax.ShapeDtypeStruct((2, t), I32),
                   jax.ShapeDtypeStruct((N_EXPERTS, LANES), I32)],
        scratch_shapes=[pltpu.VMEM((N_EXPERTS, 1), F32)],
        compiler_params=_params(("arbitrary",)),
        name="moe_router",
    )(x, g.reshape(1, d), w_router.T, b_router.reshape(N_EXPERTS, 1), upper)


def _row_copy(src, src_row, dst, dst_row, sem):
    return pltpu.make_async_copy(src.at[pl.ds(src_row, 1)], dst.at[pl.ds(dst_row, 1)], sem)


def _dispatch_body(dest_ref, xn_ref, buf_in_hbm, buf_hbm, sem, *, tb, t_total):
    del buf_in_hbm
    base = pl.program_id(0) * tb

    def issue(t, c):
        for k in range(2):
            _row_copy(xn_ref, t, buf_hbm, dest_ref[k * t_total + base + t], sem).start()
        return c

    lax.fori_loop(0, tb, issue, 0, unroll=8)

    def drain(t, c):
        for k in range(2):
            _row_copy(xn_ref, 0, buf_hbm, 0, sem).wait()
        return c

    lax.fori_loop(0, tb, drain, 0, unroll=8)


def moe_dispatch(dest_flat, xn, n_rows, tb):
    t, d = xn.shape
    buf0 = jnp.zeros((n_rows, d), xn.dtype)
    grid_spec = pltpu.PrefetchScalarGridSpec(
        num_scalar_prefetch=1,
        grid=(t // tb,),
        in_specs=[pl.BlockSpec((tb, d), lambda i, dest: (i, 0)), pl.BlockSpec(memory_space=pl.ANY)],
        out_specs=pl.BlockSpec(memory_space=pl.ANY),
        scratch_shapes=[pltpu.SemaphoreType.DMA(())],
    )
    return pl.pallas_call(
        functools.partial(_dispatch_body, tb=tb, t_total=t),
        grid_spec=grid_spec,
        out_shape=jax.ShapeDtypeStruct((n_rows, d), xn.dtype),
        input_output_aliases={2: 0},
        compiler_params=_params(("arbitrary",)),
        name="moe_dispatch",
    )(dest_flat, xn, buf0)


def _swiglu_hidden(xb, wg_ref, wu_ref, h_ref, widx):
    f = h_ref.shape[1]
    for c0 in range(0, f, MXU_TILE):
        c1 = min(c0 + MXU_TILE, f)
        gate = _dot(xb, wg_ref[widx + (slice(None), slice(c0, c1))])
        up = _dot(xb, wu_ref[widx + (slice(None), slice(c0, c1))])
        h_ref[:, c0:c1] = (gate * jax.nn.sigmoid(gate) * up).astype(BF16)


def _experts_body(be_ref, nv_ref, x_ref, wg_ref, wu_ref, wd_ref, o_ref, h_ref):
    del be_ref
    valid = pl.program_id(0) < nv_ref[0]

    @pl.when(valid)
    def _():
        _swiglu_hidden(x_ref[...].astype(BF16), wg_ref, wu_ref, h_ref, (0,))
        o_ref[...] = _dot(h_ref[...], wd_ref[0])

    @pl.when(jnp.logical_not(valid))
    def _():
        o_ref[...] = jnp.zeros_like(o_ref)


def moe_experts(block_e, n_valid, buf, wg, wu, wd, layer, tm):
    n_rows, d = buf.shape
    f = wg.shape[3]
    once = pl.Buffered(1)
    grid_spec = pltpu.PrefetchScalarGridSpec(
        num_scalar_prefetch=2,
        grid=(n_rows // tm,),
        in_specs=[pl.BlockSpec((tm, d), lambda i, be, nv: (i, 0)),
                  pl.BlockSpec((None, 1, d, f), lambda i, be, nv: (layer, be[i], 0, 0), pipeline_mode=once),
                  pl.BlockSpec((None, 1, d, f), lambda i, be, nv: (layer, be[i], 0, 0), pipeline_mode=once),
                  pl.BlockSpec((None, 1, f, d), lambda i, be, nv: (layer, be[i], 0, 0), pipeline_mode=once)],
        out_specs=pl.BlockSpec((tm, d), lambda i, be, nv: (i, 0)),
        scratch_shapes=[pltpu.VMEM((tm, f), BF16)],
    )
    return pl.pallas_call(
        _experts_body,
        grid_spec=grid_spec,
        out_shape=jax.ShapeDtypeStruct((n_rows, d), F32),
        compiler_params=_params(("arbitrary",), vmem=VMEM_LIMIT_BIG),
        name="moe_experts",
    )(block_e, n_valid, buf, wg, wu, wd)


def _combine_body(dest_ref, x_ref, gate_ref, eo_hbm, o_ref, buf_ref, sem, *, tb, t_total):
    base = pl.program_id(0) * tb

    def issue(t, c):
        for k in range(2):
            _row_copy(eo_hbm, dest_ref[k * t_total + base + t], buf_ref.at[k], t, sem).start()
        return c

    lax.fori_loop(0, tb, issue, 0, unroll=8)

    def drain(t, c):
        for k in range(2):
            _row_copy(eo_hbm, 0, buf_ref.at[k], 0, sem).wait()
        return c

    lax.fori_loop(0, tb, drain, 0, unroll=8)
    gates = gate_ref[...]
    o_ref[...] = x_ref[...] + gates[:, 0:1] * buf_ref[0] + gates[:, 1:2] * buf_ref[1]


def moe_combine(dest_flat, x, gates_t, expert_out, tb):
    t, d = x.shape
    grid_spec = pltpu.PrefetchScalarGridSpec(
        num_scalar_prefetch=1,
        grid=(t // tb,),
        in_specs=[pl.BlockSpec((tb, d), lambda i, dest: (i, 0)),
                  pl.BlockSpec((tb, 2), lambda i, dest: (i, 0)),
                  pl.BlockSpec(memory_space=pl.ANY)],
        out_specs=pl.BlockSpec((tb, d), lambda i, dest: (i, 0)),
        scratch_shapes=[pltpu.VMEM((2, tb, d), F32), pltpu.SemaphoreType.DMA(())],
    )
    return pl.pallas_call(
        functools.partial(_combine_body, tb=tb, t_total=t),
        grid_spec=grid_spec,
        out_shape=jax.ShapeDtypeStruct((t, d), F32),
        compiler_params=_params(("arbitrary",)),
        name="moe_combine",
    )(dest_flat, x, gates_t, expert_out)


def moe_layer(x, g, w_router, b_router, wg, wu, wd, layer, tiles):
    t, d = x.shape
    tm = tiles["moe_tm"]
    xn, idx, gates, rank, cnt = moe_router(x, g, w_router, b_router, tiles["router_tm"])
    counts = cnt[:, 0]
    padded = ((counts + tm - 1) // tm) * tm
    pends = jnp.cumsum(padded)
    pstarts = pends - padded
    group_start = sum(jnp.where(idx == e, pstarts[e], 0) for e in range(N_EXPERTS))
    dest_flat = (group_start + rank).reshape(-1)
    n_rows = 2 * t + N_EXPERTS * tm
    n_blocks = n_rows // tm
    block_row = jnp.arange(n_blocks, dtype=I32) * tm
    block_e = jnp.minimum(sum((block_row >= pends[e]).astype(I32) for e in range(N_EXPERTS)),
                          N_EXPERTS - 1)
    n_valid = (pends[-1:] // tm).astype(I32)
    buf = moe_dispatch(dest_flat, xn, n_rows, tiles["moe_tb"])
    eo = moe_experts(block_e, n_valid, buf, wg, wu, wd, layer, tm)
    return moe_combine(dest_flat, x, gates.T, eo, tiles["moe_tb"])


def _tiles(t, s):
    def fit(n, want):
        while n % want:
            want //= 2
        return want
    return dict(
        proj_tm=fit(t, 1024), attn_tq=fit(s, 256), mla_tq=fit(s, 512), prep_tm=fit(s, 512),
        ffn_tm=fit(t, 512), router_tm=fit(t, 512), moe_tm=fit(t, 512), moe_tb=fit(t, 256),
    )


def _rope_tables(seq):
    pos = jnp.arange(seq, dtype=F32)
    inv_freq = ROPE_THETA ** (-jnp.arange(0, ROPE_DIM, 2, dtype=F32) / ROPE_DIM)
    ang = pos[:, None] * inv_freq[None, :]
    ang = jnp.concatenate([ang, ang], axis=-1)
    reps = LANES // ROPE_DIM
    return jnp.tile(jnp.cos(ang), (1, reps)), jnp.tile(jnp.sin(ang), (1, reps))


def kernel(x, g_mix, g_ffn, sb_w_qkv, sb_w_o, kv_g_src, kv_w_a, kv_g_a, kv_w_b, kv_g_k_nope, kv_g_k_pe, mla_w_q_a, mla_g_q_a, mla_w_q_b, mla_g_q_nope, mla_g_q_pe, mla_w_o, ffn_w_gate, ffn_w_up, ffn_w_down, moe_w_router, moe_b_router, moe_w_gate, moe_w_up, moe_w_down):
    b, s, d = x.shape
    t = b * s
    depth = g_mix.shape[0]
    n_a = sb_w_qkv.shape[0]
    tiles = _tiles(t, s)
    cos_t, sin_t = _rope_tables(s)
    h = x.reshape(t, d)
    hw = N_HEADS * SB_HEAD_DIM
    kcat = vcat = None
    ffn_w = [w.astype(BF16) for w in (ffn_w_gate, ffn_w_up, ffn_w_down)]
    moe_w = [w.astype(BF16) for w in (moe_w_gate, moe_w_up, moe_w_down)]

    for i in range(depth):
        if i < n_a:
            col_scale = jnp.concatenate([jnp.full((hw,), LOG2E / math.sqrt(SB_HEAD_DIM), F32),
                                         jnp.ones((2 * hw,), F32)])
            w_qkv = (sb_w_qkv[i] * col_scale).astype(BF16)
            qkv = rms_matmul(h, g_mix[i], w_qkv, BF16, tiles["ffn_tm"], w_qkv.shape[1])
            att = sb_attention(qkv.reshape(b, s, 3 * hw), tiles["attn_tq"])
            h = matmul_residual(att.reshape(t, hw), sb_w_o[i].astype(BF16), h, tiles["proj_tm"])
        else:
            j = i - n_a
            if kcat is None:
                pad = jnp.zeros((d, LANES - ROPE_DIM), F32)
                w_a = jnp.concatenate([kv_w_a, pad], axis=1).astype(BF16)
                kva = rms_matmul(h, kv_g_src, w_a, F32, tiles["proj_tm"], w_a.shape[1])
                w_b = kv_w_b.reshape(KV_RANK, N_HEADS, NOPE_DIM + V_DIM)
                wn = w_b[:, :, :NOPE_DIM].reshape(KV_RANK, N_HEADS * NOPE_DIM).astype(BF16)
                wv = w_b[:, :, NOPE_DIM:].reshape(KV_RANK, N_HEADS * V_DIM).astype(BF16)
                g_pe_pad = jnp.concatenate([kv_g_k_pe, jnp.zeros((LANES - ROPE_DIM,), F32)]).reshape(1, LANES)
                kcat, vcat = kv_prep(kva, kv_g_a.reshape(1, KV_RANK), g_pe_pad, cos_t, sin_t, wn, wv,
                                     jnp.tile(kv_g_k_nope, N_HEADS).reshape(1, -1), tiles["prep_tm"], s)
                kcat = kcat.reshape(b, s, -1)
                vcat = vcat.reshape(b, s, -1)
            qa = rms_matmul(h, g_mix[i], mla_w_q_a[j].astype(BF16), F32, tiles["proj_tm"], mla_w_q_a.shape[2])
            w_qb = mla_w_q_b[j].reshape(-1, N_HEADS, NOPE_DIM + ROPE_DIM)
            wqn = w_qb[:, :, :NOPE_DIM].reshape(-1, N_HEADS * NOPE_DIM).astype(BF16)
            wqp = w_qb[:, :, NOPE_DIM:].reshape(-1, N_HEADS * ROPE_DIM).astype(BF16)
            scale = LOG2E / math.sqrt(NOPE_DIM + ROPE_DIM)
            qcat = q_prep(qa, mla_g_q_a[j].reshape(1, -1), wqn, wqp,
                          jnp.tile(mla_g_q_nope[j], N_HEADS).reshape(1, -1),
                          jnp.tile(mla_g_q_pe[j], N_HEADS).reshape(1, -1),
                          cos_t, sin_t, tiles["prep_tm"], s, scale)
            att = mla_attention(qcat.reshape(b, s, -1), kcat, vcat, tiles["mla_tq"])
            h = matmul_residual(att.reshape(t, -1), mla_w_o[j].astype(BF16), h, tiles["proj_tm"])

        m = i // 2
        if i % 2 == 0:
            h = dense_ffn(h, g_ffn[i], *ffn_w, m, tiles["ffn_tm"])
        else:
            h = moe_layer(h, g_ffn[i], moe_w_router[m], moe_b_router[m], *moe_w, m, tiles)
    return h.reshape(b, s, d)
```

```python
import functools
import math

import jax
import jax.numpy as jnp
from jax import lax
from jax.experimental import pallas as pl
from jax.experimental.pallas import tpu as pltpu

F32 = jnp.float32
BF16 = jnp.bfloat16
I32 = jnp.int32

EPS = 1e-6
N_HEADS = 16
SB_HEAD_DIM = 64
NOPE_DIM = 64
ROPE_DIM = 32
V_DIM = 64
KV_RANK = 256
ROPE_THETA = 10000.0
N_EXPERTS = 8
LANES = 128
PAIR_W = 2 * NOPE_DIM + LANES
ONES_LANE = 2 * NOPE_DIM + 2 * ROPE_DIM
MLA_STATIC_SHIFT_MAX = 50.0
MLA_GROUP = 4
LOG2E = 1.4426950408889634
SB_SKIP_EXP = 110.0
SB_SUBS = 4
MXU_TILE = 256
VMEM_LIMIT = 52 * 1024 * 1024
VMEM_LIMIT_BIG = 58 * 1024 * 1024


def _params(sem, vmem=VMEM_LIMIT):
    return pltpu.CompilerParams(dimension_semantics=sem, vmem_limit_bytes=vmem)


def _dot(a, b):
    return jnp.dot(a, b, preferred_element_type=F32)


def _dot_nt(a, b):
    return lax.dot_general(a, b, (((1,), (1,)), ((), ())), preferred_element_type=F32)


def _split_bf16(x):
    hi = x.astype(BF16)
    lo = (x - hi.astype(F32)).astype(BF16)
    return hi, lo


def _dot_hilo(x, w):
    hi, lo = _split_bf16(x)
    return _dot(hi, w) + _dot(lo, w)


def _rms_matmul_body(x_ref, g_ref, w_ref, o_ref, xn_ref):
    @pl.when(pl.program_id(1) == 0)
    def _():
        x = x_ref[...]
        inv = lax.rsqrt(jnp.mean(x * x, axis=-1, keepdims=True) + EPS)
        xn_ref[...] = (x * inv * g_ref[...]).astype(BF16)

    o_ref[...] = _dot(xn_ref[...], w_ref[...]).astype(o_ref.dtype)


def rms_matmul(x, g, w, out_dtype, tm, tn):
    t, d = x.shape
    n = w.shape[1]
    w_mode = pl.Buffered(1) if tn == n else None
    return pl.pallas_call(
        _rms_matmul_body,
        grid=(t // tm, n // tn),
        in_specs=[pl.BlockSpec((tm, d), lambda i, j: (i, 0)),
                  pl.BlockSpec((1, d), lambda i, j: (0, 0)),
                  pl.BlockSpec((d, tn), lambda i, j: (0, j), pipeline_mode=w_mode)],
        out_specs=pl.BlockSpec((tm, tn), lambda i, j: (i, j)),
        out_shape=jax.ShapeDtypeStruct((t, n), out_dtype),
        scratch_shapes=[pltpu.VMEM((tm, d), BF16)],
        compiler_params=_params(("parallel", "arbitrary")),
        name="rms_matmul",
    )(x, g.reshape(1, d), w)


def _softplus2(z):
    return jnp.maximum(jnp.log2(1.0 + jnp.exp2(jnp.minimum(z, 64.0))), z)


def _key_norm_max(k_ref, head_masks, kmax_ref):
    k = k_ref[0].astype(F32)
    k2 = k * k
    for hh, mask in enumerate(head_masks):
        n2 = jnp.sum(jnp.where(mask, k2, 0.0), axis=-1, keepdims=True)
        kmax_ref[hh] = jnp.max(n2, axis=0, keepdims=True)


def _logit_bound(qm, kmax):
    qf = qm.astype(F32)
    return jnp.sqrt(jnp.sum(qf * qf, axis=-1, keepdims=True) * kmax) * 1.01 + 1e-2


def _sb_attn_body(q_ref, k_ref, v_ref, tri_ref, o_ref, kmax_ref, qs_ref, zb_ref, carry_ref, acc_ref,
                  *, tq):
    step = pl.program_id(2)
    lane = lax.broadcasted_iota(I32, (1, LANES), 1)
    head_masks = [(lane // SB_HEAD_DIM) == hh for hh in range(2)]

    @pl.when(step == 0)
    def _():
        _key_norm_max(k_ref, head_masks, kmax_ref)

    for sub in range(SB_SUBS):
        q2 = q_ref[0, sub * tq:(sub + 1) * tq, :]
        for hh in range(2):
            qm = jnp.where(head_masks[hh], q2, jnp.zeros_like(q2))
            qs_ref[2 * sub + hh] = qm
            zb_ref[2 * sub + hh] = _logit_bound(qm, kmax_ref[hh])

    row = lax.broadcasted_iota(I32, (tq, tq), 0)
    col = lax.broadcasted_iota(I32, (tq, tq), 1)
    causal = col < row

    def chain(c, start, masked):
        z = _dot_nt(qs_ref[c], k_ref[0, pl.ds(start, tq), :])
        sp = _softplus2(z)
        if masked:
            sp = jnp.where(causal, sp, 0.0)
        csum = _dot(sp.astype(BF16), tri_ref[...])
        e = jnp.exp2(z - csum)
        if masked:
            e = jnp.where(causal, e, 0.0)
        return _dot(e.astype(BF16), v_ref[0, pl.ds(start, tq), :]), csum[:, 0:1]

    for sub in range(SB_SUBS):
        qi = SB_SUBS * step + sub
        has_prev = qi >= 1
        prev_start = pl.multiple_of(jnp.maximum(qi - 1, 0) * tq, tq)
        for hh in range(2):
            c = 2 * sub + hh
            pv_d, tot_d = chain(c, pl.multiple_of(qi * tq, tq), True)
            pv_p, tot_p = chain(c, prev_start, False)
            acc_ref[c] = pv_d + jnp.where(has_prev, jnp.exp2(-tot_d), 0.0) * pv_p
            carry_ref[c] = tot_d + jnp.where(has_prev, tot_p, 0.0)

    for sub in range(SB_SUBS):
        qi = SB_SUBS * step + sub
        chains = slice(2 * sub, 2 * sub + 2)

        def cond(jj):
            live = jnp.max(zb_ref[chains] - carry_ref[chains]) > -SB_SKIP_EXP * LOG2E
            return jnp.logical_and(jj <= qi, live)

        def body(jj):
            start = pl.multiple_of((qi - jj) * tq, tq)
            for c in range(2 * sub, 2 * sub + 2):
                pv, tot = chain(c, start, False)
                carry = carry_ref[c]
                acc_ref[c] += jnp.exp2(-carry) * pv
                carry_ref[c] = carry + tot
            return jj + 1

        lax.while_loop(cond, body, jnp.int32(2))
        o_ref[0, sub * tq:(sub + 1) * tq, :] = jnp.where(
            head_masks[0], acc_ref[2 * sub], acc_ref[2 * sub + 1]).astype(o_ref.dtype)


def sb_attention(qkv, tq):
    b, s, _ = qkv.shape
    n_pairs = N_HEADS // 2
    rows = SB_SUBS * tq
    n_chains = 2 * SB_SUBS
    ji = lax.broadcasted_iota(I32, (tq, tq), 0)
    si = lax.broadcasted_iota(I32, (tq, tq), 1)
    tri = (ji >= si).astype(BF16)
    return pl.pallas_call(
        functools.partial(_sb_attn_body, tq=tq),
        grid=(b, n_pairs, s // rows),
        in_specs=[pl.BlockSpec((1, rows, LANES), lambda bi, p, i: (bi, i, p)),
                  pl.BlockSpec((1, s, LANES), lambda bi, p, i: (bi, 0, n_pairs + p)),
                  pl.BlockSpec((1, s, LANES), lambda bi, p, i: (bi, 0, 2 * n_pairs + p)),
                  pl.BlockSpec((tq, tq), lambda bi, p, i: (0, 0))],
        out_specs=pl.BlockSpec((1, rows, LANES), lambda bi, p, i: (bi, i, p)),
        out_shape=jax.ShapeDtypeStruct((b, s, N_HEADS * SB_HEAD_DIM), BF16),
        scratch_shapes=[pltpu.VMEM((2, 1, 1), F32),
                        pltpu.VMEM((n_chains, tq, LANES), BF16),
                        pltpu.VMEM((n_chains, tq, 1), F32),
                        pltpu.VMEM((n_chains, tq, 1), F32),
                        pltpu.VMEM((n_chains, tq, LANES), F32)],
        compiler_params=_params(("parallel", "parallel", "arbitrary")),
        name="sb_attention",
    )(qkv, qkv, qkv, tri)


def _mla_attn_body(q_ref, k_ref, v_ref, o_ref, kmax_ref, qs_ref, m_ref, acc_ref, *, tq):
    qi = pl.program_id(2)
    lane_q = lax.broadcasted_iota(I32, (1, PAIR_W), 1)
    lane_o = lax.broadcasted_iota(I32, (1, LANES), 1)
    head_masks = [jnp.logical_or((lane_q // NOPE_DIM) == hh,
                                 jnp.logical_and(lane_q >= 2 * NOPE_DIM,
                                                 ((lane_q - 2 * NOPE_DIM) // ROPE_DIM) == hh))
                  for hh in range(2)]

    @pl.when(qi == 0)
    def _():
        _key_norm_max(k_ref, head_masks, kmax_ref)

    q2 = q_ref[0]
    shift_max = jnp.float32(0.0)
    for hh in range(2):
        qm = jnp.where(head_masks[hh], q2, jnp.zeros_like(q2))
        bound = _logit_bound(qm, kmax_ref[hh])
        qs_ref[hh] = jnp.where(lane_q == ONES_LANE, (-bound).astype(BF16), qm)
        shift_max = jnp.maximum(shift_max, jnp.max(bound))
        acc_ref[hh] = jnp.zeros((tq, 2 * LANES), F32)

    ones = jnp.ones((tq, LANES), BF16)
    causal = (lax.broadcasted_iota(I32, (tq, tq), 1) <= lax.broadcasted_iota(I32, (tq, tq), 0))

    def scores(hh, j, masked):
        sc = _dot_nt(qs_ref[hh], k_ref[0, pl.ds(pl.multiple_of(j * tq, tq), tq), :])
        return jnp.where(causal, sc, -jnp.inf) if masked else sc

    def values(j):
        return jnp.concatenate([v_ref[0, pl.ds(pl.multiple_of(j * tq, tq), tq), :], ones], axis=-1)

    def static_blocks(blocks):
        for hh in range(2):
            acc_ref[hh] += sum(_dot(jnp.exp2(scores(hh, j, masked)).astype(BF16), values(j))
                               for j, masked in blocks)

    def online_block(j, masked):
        for hh in range(2):
            sc = scores(hh, j, masked)
            m_prev = m_ref[hh]
            m_new = jnp.maximum(m_prev, jnp.max(sc, axis=-1, keepdims=True))
            p = jnp.exp2(sc - m_new)
            acc_ref[hh] = jnp.exp2(m_prev - m_new) * acc_ref[hh] + _dot(p.astype(BF16), values(j))
            m_ref[hh] = m_new

    @pl.when(shift_max <= MLA_STATIC_SHIFT_MAX)
    def _():
        def group(j, c):
            static_blocks([(MLA_GROUP * j + u, False) for u in range(MLA_GROUP)])
            return c
        lax.fori_loop(0, qi // MLA_GROUP, group, 0)

        for rest in range(MLA_GROUP):
            @pl.when(qi % MLA_GROUP == rest)
            def _():
                static_blocks([(qi - rest + u, False) for u in range(rest)] + [(qi, True)])

    @pl.when(shift_max > MLA_STATIC_SHIFT_MAX)
    def _():
        m_ref[...] = jnp.full_like(m_ref, -jnp.inf)

        def body(j, c):
            online_block(j, False)
            return c
        lax.fori_loop(0, qi, body, 0)
        online_block(qi, True)

    outs = [acc_ref[hh][:, :LANES] / acc_ref[hh][:, LANES:] for hh in range(2)]
    o_ref[0] = jnp.where(lane_o < V_DIM, outs[0], outs[1]).astype(o_ref.dtype)


def mla_attention(qcat, kcat, v, tq):
    b, s, _ = qcat.shape
    n_pairs = N_HEADS // 2
    return pl.pallas_call(
        functools.partial(_mla_attn_body, tq=tq),
        grid=(b, n_pairs, s // tq),
        in_specs=[pl.BlockSpec((1, tq, PAIR_W), lambda bi, p, i: (bi, i, p)),
                  pl.BlockSpec((1, s, PAIR_W), lambda bi, p, i: (bi, 0, p)),
                  pl.BlockSpec((1, s, LANES), lambda bi, p, i: (bi, 0, p))],
        out_specs=pl.BlockSpec((1, tq, LANES), lambda bi, p, i: (bi, i, p)),
        out_shape=jax.ShapeDtypeStruct((b, s, N_HEADS * V_DIM), BF16),
        scratch_shapes=[pltpu.VMEM((2, 1, 1), F32),
                        pltpu.VMEM((2, tq, PAIR_W), BF16),
                        pltpu.VMEM((2, tq, 1), F32),
                        pltpu.VMEM((2, tq, 2 * LANES), F32)],
        compiler_params=_params(("parallel", "parallel", "arbitrary")),
        name="mla_attention",
    )(qcat, kcat, v)


def _group_rms(x, gmat, group, g):
    chunks = []
    for c in range(x.shape[1] // 256):
        xc = x[:, c * 256:(c + 1) * 256]
        ss = _dot_hilo(xc * xc, gmat)
        chunks.append(xc * lax.rsqrt(ss * (1.0 / group) + EPS))
    y = chunks[0] if len(chunks) == 1 else jnp.concatenate(chunks, axis=-1)
    return y * g


def _rope128(x, cos, sin):
    lane = lax.broadcasted_iota(I32, (1, LANES), 1)
    first_half = (lane % ROPE_DIM) < (ROPE_DIM // 2)
    rot = jnp.where(first_half,
                    -pltpu.roll(x, LANES - ROPE_DIM // 2, 1),
                    pltpu.roll(x, ROPE_DIM // 2, 1))
    return x * cos + rot * sin


def _kv_prep_body(kva_ref, g_a_ref, g_pe_ref, cos_ref, sin_ref, wn_ref, wv_ref, g_n_ref,
                  g64_ref, k_ref, v_ref):
    lane = lax.broadcasted_iota(I32, (1, LANES), 1)
    kva = kva_ref[...]
    c = kva[:, :KV_RANK]
    cn = (c * lax.rsqrt(jnp.mean(c * c, axis=-1, keepdims=True) + EPS) * g_a_ref[...]).astype(BF16)
    pe = kva[:, KV_RANK:KV_RANK + LANES]
    ms = jnp.sum(pe * pe, axis=-1, keepdims=True) * (1.0 / ROPE_DIM)
    pen = pe * lax.rsqrt(ms + EPS) * g_pe_ref[...]
    kpe = _rope128(pen, cos_ref[...], sin_ref[...])
    kpe = jnp.where(lane < ROPE_DIM, kpe, 0.0)
    kpe2 = kpe + pltpu.roll(kpe, ROPE_DIM, 1)
    kpe2 = jnp.where(lane == ONES_LANE - 2 * NOPE_DIM, 1.0, kpe2).astype(BF16)
    kn = _group_rms(_dot(cn, wn_ref[...]), g64_ref[...], NOPE_DIM, g_n_ref[...]).astype(BF16)
    v_ref[...] = _dot(cn, wv_ref[...]).astype(BF16)
    for p in range(N_HEADS // 2):
        k_ref[:, p * PAIR_W:p * PAIR_W + LANES] = kn[:, p * LANES:(p + 1) * LANES]
        k_ref[:, p * PAIR_W + LANES:(p + 1) * PAIR_W] = kpe2


def _q_prep_body(qa_ref, g_a_ref, wn_ref, wp_ref, g_n_ref, g_p_ref, cos_ref, sin_ref,
                 g64_ref, g32_ref, q_ref, *, scale):
    lane = lax.broadcasted_iota(I32, (1, LANES), 1)
    qa = qa_ref[...]
    qan = (qa * lax.rsqrt(jnp.mean(qa * qa, axis=-1, keepdims=True) + EPS) * g_a_ref[...]).astype(BF16)
    qn = _group_rms(_dot(qan, wn_ref[...]), g64_ref[...], NOPE_DIM, g_n_ref[...]) * scale
    qp = _group_rms(_dot(qan, wp_ref[...]), g32_ref[...], ROPE_DIM, g_p_ref[...])
    cos = cos_ref[...]
    sin = sin_ref[...]
    for c in range(N_HEADS * ROPE_DIM // LANES):
        pe4 = _rope128(qp[:, c * LANES:(c + 1) * LANES], cos, sin) * scale
        for half in range(2):
            p = 2 * c + half
            x = pe4 if half == 0 else pltpu.roll(pe4, 2 * ROPE_DIM, 1)
            q_ref[:, p * PAIR_W:p * PAIR_W + LANES] = qn[:, p * LANES:(p + 1) * LANES].astype(BF16)
            q_ref[:, p * PAIR_W + LANES:(p + 1) * PAIR_W] = jnp.where(lane < 2 * ROPE_DIM, x, 0.0).astype(BF16)


def _block_diag_ones(n, group):
    a = lax.broadcasted_iota(I32, (n, n), 0) // group
    b = lax.broadcasted_iota(I32, (n, n), 1) // group
    return (a == b).astype(BF16)


def _full(shape):
    return pl.BlockSpec(shape, lambda i: (0,) * len(shape))


def kv_prep(kva, g_a, g_pe_pad, cos_t, sin_t, wn, wv, g_n_t, tm, seq):
    t = kva.shape[0]
    nseq = seq // tm
    return pl.pallas_call(
        _kv_prep_body,
        grid=(t // tm,),
        in_specs=[pl.BlockSpec((tm, kva.shape[1]), lambda i: (i, 0)),
                  _full((1, KV_RANK)), _full((1, LANES)),
                  pl.BlockSpec((tm, LANES), lambda i: (i % nseq, 0)),
                  pl.BlockSpec((tm, LANES), lambda i: (i % nseq, 0)),
                  _full(wn.shape), _full(wv.shape), _full((1, wn.shape[1])),
                  _full((256, 256))],
        out_specs=[pl.BlockSpec((tm, (N_HEADS // 2) * PAIR_W), lambda i: (i, 0)),
                   pl.BlockSpec((tm, N_HEADS * V_DIM), lambda i: (i, 0))],
        out_shape=[jax.ShapeDtypeStruct((t, (N_HEADS // 2) * PAIR_W), BF16),
                   jax.ShapeDtypeStruct((t, N_HEADS * V_DIM), BF16)],
        compiler_params=_params(("parallel",)),
        name="kv_prep",
    )(kva, g_a, g_pe_pad, cos_t, sin_t, wn, wv, g_n_t, _block_diag_ones(256, NOPE_DIM))


def q_prep(qa, g_a, wn, wp, g_n_t, g_p_t, cos_t, sin_t, tm, seq, scale):
    t = qa.shape[0]
    nseq = seq // tm
    return pl.pallas_call(
        functools.partial(_q_prep_body, scale=scale),
        grid=(t // tm,),
        in_specs=[pl.BlockSpec((tm, qa.shape[1]), lambda i: (i, 0)),
                  _full((1, qa.shape[1])), _full(wn.shape), _full(wp.shape),
                  _full((1, wn.shape[1])), _full((1, wp.shape[1])),
                  pl.BlockSpec((tm, LANES), lambda i: (i % nseq, 0)),
                  pl.BlockSpec((tm, LANES), lambda i: (i % nseq, 0)),
                  _full((256, 256)), _full((256, 256))],
        out_specs=pl.BlockSpec((tm, (N_HEADS // 2) * PAIR_W), lambda i: (i, 0)),
        out_shape=jax.ShapeDtypeStruct((t, (N_HEADS // 2) * PAIR_W), BF16),
        compiler_params=_params(("parallel",)),
        name="q_prep",
    )(qa, g_a, wn, wp, g_n_t, g_p_t, cos_t, sin_t,
      _block_diag_ones(256, NOPE_DIM), _block_diag_ones(256, ROPE_DIM))


def _ffn_body(att_ref, wo_ref, res_ref, g_ref, wg_ref, wu_ref, wd_ref, o_ref, h_ref):
    x = res_ref[...] + _dot(att_ref[...], wo_ref[...])
    inv = lax.rsqrt(jnp.mean(x * x, axis=-1, keepdims=True) + EPS)
    xn = (x * inv * g_ref[...]).astype(BF16)
    _swiglu_hidden(xn, wg_ref, wu_ref, h_ref, ())
    o_ref[...] = x + _dot(h_ref[...], wd_ref[...])


def dense_ffn(att, wo, res, g, wg, wu, wd, layer, tm):
    t, d = res.shape
    f = wg.shape[2]
    once = pl.Buffered(1)
    return pl.pallas_call(
        _ffn_body,
        grid=(t // tm,),
        in_specs=[pl.BlockSpec((tm, att.shape[1]), lambda i: (i, 0)),
                  pl.BlockSpec(wo.shape, lambda i: (0, 0), pipeline_mode=once),
                  pl.BlockSpec((tm, d), lambda i: (i, 0)),
                  pl.BlockSpec((1, d), lambda i: (0, 0)),
                  pl.BlockSpec((None, d, f), lambda i: (layer, 0, 0), pipeline_mode=once),
                  pl.BlockSpec((None, d, f), lambda i: (layer, 0, 0), pipeline_mode=once),
                  pl.BlockSpec((None, f, d), lambda i: (layer, 0, 0), pipeline_mode=once)],
        out_specs=pl.BlockSpec((tm, d), lambda i: (i, 0)),
        out_shape=jax.ShapeDtypeStruct((t, d), F32),
        scratch_shapes=[pltpu.VMEM((tm, f), BF16)],
        compiler_params=_params(("parallel",)),
        name="dense_ffn",
    )(att, wo, res, g.reshape(1, d), wg, wu, wd)


def _router_body(att_ref, wo_ref, res_ref, g_ref, wr_ref, b_ref, upper_ref,
                 x_ref, xn_ref, idx_ref, gate_ref, rank_ref, cnt_ref, base_ref):
    i = pl.program_id(0)

    @pl.when(i == 0)
    def _():
        base_ref[...] = jnp.zeros_like(base_ref)

    x = res_ref[...] + _dot(att_ref[...], wo_ref[...])
    x_ref[...] = x
    xn = x * lax.rsqrt(jnp.mean(x * x, axis=-1, keepdims=True) + EPS) * g_ref[...]
    xn_ref[...] = xn
    tm = x.shape[0]

    x_hi, x_lo = _split_bf16(xn)
    w_hi, w_lo = _split_bf16(wr_ref[...])
    logits = _dot_nt(w_hi, x_hi) + _dot_nt(w_hi, x_lo) + _dot_nt(w_lo, x_hi) + b_ref[...]

    e_iota = lax.broadcasted_iota(I32, (N_EXPERTS, tm), 0)
    m1 = jnp.max(logits, axis=0, keepdims=True)
    i1 = jnp.min(jnp.where(logits == m1, e_iota, N_EXPERTS), axis=0, keepdims=True)
    sel1 = e_iota == i1
    rest = jnp.where(sel1, -jnp.inf, logits)
    m2 = jnp.max(rest, axis=0, keepdims=True)
    i2 = jnp.min(jnp.where(rest == m2, e_iota, N_EXPERTS), axis=0, keepdims=True)
    sel2 = e_iota == i2
    e2 = jnp.exp(m2 - m1)
    g1 = 1.0 / (1.0 + e2)
    idx_ref[...] = jnp.concatenate([i1, i2], axis=0)
    gate_ref[...] = jnp.concatenate([g1, e2 * g1], axis=0)

    member = jnp.logical_or(sel1, sel2)
    prefix = _dot(member.astype(BF16), upper_ref[...])
    rank = prefix + base_ref[...]
    r1 = jnp.sum(jnp.where(sel1, rank, 0.0), axis=0, keepdims=True)
    r2 = jnp.sum(jnp.where(sel2, rank, 0.0), axis=0, keepdims=True)
    rank_ref[...] = jnp.concatenate([r1, r2], axis=0).astype(I32)
    base_ref[...] += jnp.sum(member.astype(F32), axis=1, keepdims=True)
    cnt_ref[...] = jnp.broadcast_to(base_ref[...], cnt_ref.shape).astype(I32)


def moe_router(att, wo, res, g, w_router, b_router, tm):
    t, d = res.shape
    a = lax.broadcasted_iota(I32, (tm, tm), 0)
    b = lax.broadcasted_iota(I32, (tm, tm), 1)
    upper = (a < b).astype(BF16)
    return pl.pallas_call(
        _router_body,
        grid=(t // tm,),
        in_specs=[pl.BlockSpec((tm, att.shape[1]), lambda i: (i, 0)), _full(wo.shape),
                  pl.BlockSpec((tm, d), lambda i: (i, 0)),
                  _full((1, d)), _full((N_EXPERTS, d)), _full((N_EXPERTS, 1)), _full((tm, tm))],
        out_specs=[pl.BlockSpec((tm, d), lambda i: (i, 0)),
                   pl.BlockSpec((tm, d), lambda i: (i, 0)),
                   pl.BlockSpec((2, tm), lambda i: (0, i)),
                   pl.BlockSpec((2, tm), lambda i: (0, i)),
                   pl.BlockSpec((2, tm), lambda i: (0, i)),
                   _full((N_EXPERTS, LANES))],
        out_shape=[jax.ShapeDtypeStruct((t, d), F32),
                   jax.ShapeDtypeStruct((t, d), F32),
                   jax.ShapeDtypeStruct((2, t), I32),
                   jax.ShapeDtypeStruct((2, t), F32),
                   jax.ShapeDtypeStruct((2, t), I32),
                   jax.ShapeDtypeStruct((N_EXPERTS, LANES), I32)],
        scratch_shapes=[pltpu.VMEM((N_EXPERTS, 1), F32)],
        compiler_params=_params(("arbitrary",)),
        name="moe_router",
    )(att, wo, res, g.reshape(1, d), w_router.T, b_router.reshape(N_EXPERTS, 1), upper)


def _row_copy(src, src_row, dst, dst_row, sem):
    return pltpu.make_async_copy(src.at[pl.ds(src_row, 1)], dst.at[pl.ds(dst_row, 1)], sem)


def _dispatch_body(dest_ref, xn_ref, buf_in_hbm, buf_hbm, sem, *, tb, t_total):
    del buf_in_hbm
    base = pl.program_id(0) * tb

    def issue(t, c):
        for k in range(2):
            _row_copy(xn_ref, t, buf_hbm, dest_ref[k * t_total + base + t], sem).start()
        return c

    lax.fori_loop(0, tb, issue, 0, unroll=8)

    def drain(t, c):
        for k in range(2):
            _row_copy(xn_ref, 0, buf_hbm, 0, sem).wait()
        return c

    lax.fori_loop(0, tb, drain, 0, unroll=8)


def moe_dispatch(dest_flat, xn, n_rows, tb):
    t, d = xn.shape
    buf0 = jnp.zeros((n_rows, d), xn.dtype)
    grid_spec = pltpu.PrefetchScalarGridSpec(
        num_scalar_prefetch=1,
        grid=(t // tb,),
        in_specs=[pl.BlockSpec((tb, d), lambda i, dest: (i, 0)), pl.BlockSpec(memory_space=pl.ANY)],
        out_specs=pl.BlockSpec(memory_space=pl.ANY),
        scratch_shapes=[pltpu.SemaphoreType.DMA(())],
    )
    return pl.pallas_call(
        functools.partial(_dispatch_body, tb=tb, t_total=t),
        grid_spec=grid_spec,
        out_shape=jax.ShapeDtypeStruct((n_rows, d), xn.dtype),
        input_output_aliases={2: 0},
        compiler_params=_params(("arbitrary",)),
        name="moe_dispatch",
    )(dest_flat, xn, buf0)


def _swiglu_hidden(xb, wg_ref, wu_ref, h_ref, widx):
    f = h_ref.shape[1]
    for c0 in range(0, f, MXU_TILE):
        c1 = min(c0 + MXU_TILE, f)
        gate = _dot(xb, wg_ref[widx + (slice(None), slice(c0, c1))])
        up = _dot(xb, wu_ref[widx + (slice(None), slice(c0, c1))])
        h_ref[:, c0:c1] = (gate * jax.nn.sigmoid(gate) * up).astype(BF16)


def _experts_body(be_ref, nv_ref, x_ref, wg_ref, wu_ref, wd_ref, o_ref, h_ref):
    del be_ref
    valid = pl.program_id(0) < nv_ref[0]

    @pl.when(valid)
    def _():
        _swiglu_hidden(x_ref[...].astype(BF16), wg_ref, wu_ref, h_ref, (0,))
        o_ref[...] = _dot(h_ref[...], wd_ref[0])

    @pl.when(jnp.logical_not(valid))
    def _():
        o_ref[...] = jnp.zeros_like(o_ref)


def moe_experts(block_e, n_valid, buf, wg, wu, wd, layer, tm):
    n_rows, d = buf.shape
    f = wg.shape[3]
    once = pl.Buffered(1)
    grid_spec = pltpu.PrefetchScalarGridSpec(
        num_scalar_prefetch=2,
        grid=(n_rows // tm,),
        in_specs=[pl.BlockSpec((tm, d), lambda i, be, nv: (i, 0)),
                  pl.BlockSpec((None, 1, d, f), lambda i, be, nv: (layer, be[i], 0, 0), pipeline_mode=once),
                  pl.BlockSpec((None, 1, d, f), lambda i, be, nv: (layer, be[i], 0, 0), pipeline_mode=once),
                  pl.BlockSpec((None, 1, f, d), lambda i, be, nv: (layer, be[i], 0, 0), pipeline_mode=once)],
        out_specs=pl.BlockSpec((tm, d), lambda i, be, nv: (i, 0)),
        scratch_shapes=[pltpu.VMEM((tm, f), BF16)],
    )
    return pl.pallas_call(
        _experts_body,
        grid_spec=grid_spec,
        out_shape=jax.ShapeDtypeStruct((n_rows, d), F32),
        compiler_params=_params(("arbitrary",), vmem=VMEM_LIMIT_BIG),
        name="moe_experts",
    )(block_e, n_valid, buf, wg, wu, wd)


def _combine_body(dest_ref, x_ref, gate_ref, eo_hbm, o_ref, buf_ref, sems, *, tb, t_total):
    i = pl.program_id(0)
    slot = i % 2

    def start_block(block, to_slot):
        base = block * tb

        def issue(t, c):
            for k in range(2):
                _row_copy(eo_hbm, dest_ref[k * t_total + base + t], buf_ref.at[to_slot, k], t,
                          sems.at[to_slot]).start()
            return c

        lax.fori_loop(0, tb, issue, 0, unroll=8)

    @pl.when(i == 0)
    def _():
        start_block(0, 0)

    @pl.when(i + 1 < pl.num_programs(0))
    def _():
        start_block(i + 1, 1 - slot)

    def drain(t, c):
        for k in range(2):
            _row_copy(eo_hbm, 0, buf_ref.at[slot, k], 0, sems.at[slot]).wait()
        return c

    lax.fori_loop(0, tb, drain, 0, unroll=8)
    gates = gate_ref[...]
    o_ref[...] = x_ref[...] + gates[:, 0:1] * buf_ref[slot, 0] + gates[:, 1:2] * buf_ref[slot, 1]


def moe_combine(dest_flat, x, gates_t, expert_out, tb):
    t, d = x.shape
    grid_spec = pltpu.PrefetchScalarGridSpec(
        num_scalar_prefetch=1,
        grid=(t // tb,),
        in_specs=[pl.BlockSpec((tb, d), lambda i, dest: (i, 0)),
                  pl.BlockSpec((tb, 2), lambda i, dest: (i, 0)),
                  pl.BlockSpec(memory_space=pl.ANY)],
        out_specs=pl.BlockSpec((tb, d), lambda i, dest: (i, 0)),
        scratch_shapes=[pltpu.VMEM((2, 2, tb, d), F32), pltpu.SemaphoreType.DMA((2,))],
    )
    return pl.pallas_call(
        functools.partial(_combine_body, tb=tb, t_total=t),
        grid_spec=grid_spec,
        out_shape=jax.ShapeDtypeStruct((t, d), F32),
        compiler_params=_params(("arbitrary",)),
        name="moe_combine",
    )(dest_flat, x, gates_t, expert_out)


def moe_layer(att, wo, res, g, w_router, b_router, wg, wu, wd, layer, tiles):
    t, d = res.shape
    tm = tiles["moe_tm"]
    x, xn, idx, gates, rank, cnt = moe_router(att, wo, res, g, w_router, b_router, tiles["router_tm"])
    counts = cnt[:, 0]
    padded = ((counts + tm - 1) // tm) * tm
    pends = jnp.cumsum(padded)
    pstarts = pends - padded
    group_start = sum(jnp.where(idx == e, pstarts[e], 0) for e in range(N_EXPERTS))
    dest_flat = (group_start + rank).reshape(-1)
    n_rows = 2 * t + N_EXPERTS * tm
    n_blocks = n_rows // tm
    block_row = jnp.arange(n_blocks, dtype=I32) * tm
    block_e = jnp.minimum(sum((block_row >= pends[e]).astype(I32) for e in range(N_EXPERTS)),
                          N_EXPERTS - 1)
    n_valid = (pends[-1:] // tm).astype(I32)
    buf = moe_dispatch(dest_flat, xn, n_rows, tiles["moe_tb"])
    eo = moe_experts(block_e, n_valid, buf, wg, wu, wd, layer, tm)
    return moe_combine(dest_flat, x, gates.T, eo, tiles["moe_tb"])


def _tiles(t, s):
    def fit(n, want):
        while n % want:
            want //= 2
        return want
    return dict(
        proj_tm=fit(t, 1024), attn_tq=fit(s, 256), mla_tq=fit(s, 512), prep_tm=fit(s, 512),
        ffn_tm=fit(t, 512), router_tm=fit(t, 512), moe_tm=fit(t, 512), moe_tb=fit(t, 256),
    )


def _rope_tables(seq):
    pos = jnp.arange(seq, dtype=F32)
    inv_freq = ROPE_THETA ** (-jnp.arange(0, ROPE_DIM, 2, dtype=F32) / ROPE_DIM)
    ang = pos[:, None] * inv_freq[None, :]
    ang = jnp.concatenate([ang, ang], axis=-1)
    reps = LANES // ROPE_DIM
    return jnp.tile(jnp.cos(ang), (1, reps)), jnp.tile(jnp.sin(ang), (1, reps))


def kernel(x, g_mix, g_ffn, sb_w_qkv, sb_w_o, kv_g_src, kv_w_a, kv_g_a, kv_w_b, kv_g_k_nope, kv_g_k_pe, mla_w_q_a, mla_g_q_a, mla_w_q_b, mla_g_q_nope, mla_g_q_pe, mla_w_o, ffn_w_gate, ffn_w_up, ffn_w_down, moe_w_router, moe_b_router, moe_w_gate, moe_w_up, moe_w_down):
    b, s, d = x.shape
    t = b * s
    depth = g_mix.shape[0]
    n_a = sb_w_qkv.shape[0]
    tiles = _tiles(t, s)
    cos_t, sin_t = _rope_tables(s)
    h = x.reshape(t, d)
    hw = N_HEADS * SB_HEAD_DIM
    kcat = vcat = None
    ffn_w = [w.astype(BF16) for w in (ffn_w_gate, ffn_w_up, ffn_w_down)]
    moe_w = [w.astype(BF16) for w in (moe_w_gate, moe_w_up, moe_w_down)]

    for i in range(depth):
        if i < n_a:
            col_scale = jnp.concatenate([jnp.full((hw,), LOG2E / math.sqrt(SB_HEAD_DIM), F32),
                                         jnp.ones((2 * hw,), F32)])
            w_qkv = (sb_w_qkv[i] * col_scale).astype(BF16)
            qkv = rms_matmul(h, g_mix[i], w_qkv, BF16, tiles["ffn_tm"], w_qkv.shape[1])
            att = sb_attention(qkv.reshape(b, s, 3 * hw), tiles["attn_tq"])
            w_o = sb_w_o[i].astype(BF16)
        else:
            j = i - n_a
            if kcat is None:
                pad = jnp.zeros((d, LANES - ROPE_DIM), F32)
                w_a = jnp.concatenate([kv_w_a, pad], axis=1).astype(BF16)
                kva = rms_matmul(h, kv_g_src, w_a, F32, tiles["proj_tm"], w_a.shape[1])
                w_b = kv_w_b.reshape(KV_RANK, N_HEADS, NOPE_DIM + V_DIM)
                wn = w_b[:, :, :NOPE_DIM].reshape(KV_RANK, N_HEADS * NOPE_DIM).astype(BF16)
                wv = w_b[:, :, NOPE_DIM:].reshape(KV_RANK, N_HEADS * V_DIM).astype(BF16)
                g_pe_pad = jnp.concatenate([kv_g_k_pe, jnp.zeros((LANES - ROPE_DIM,), F32)]).reshape(1, LANES)
                kcat, vcat = kv_prep(kva, kv_g_a.reshape(1, KV_RANK), g_pe_pad, cos_t, sin_t, wn, wv,
                                     jnp.tile(kv_g_k_nope, N_HEADS).reshape(1, -1), tiles["prep_tm"], s)
                kcat = kcat.reshape(b, s, -1)
                vcat = vcat.reshape(b, s, -1)
            qa = rms_matmul(h, g_mix[i], mla_w_q_a[j].astype(BF16), F32, tiles["proj_tm"], mla_w_q_a.shape[2])
            w_qb = mla_w_q_b[j].reshape(-1, N_HEADS, NOPE_DIM + ROPE_DIM)
            wqn = w_qb[:, :, :NOPE_DIM].reshape(-1, N_HEADS * NOPE_DIM).astype(BF16)
            wqp = w_qb[:, :, NOPE_DIM:].reshape(-1, N_HEADS * ROPE_DIM).astype(BF16)
            scale = LOG2E / math.sqrt(NOPE_DIM + ROPE_DIM)
            qcat = q_prep(qa, mla_g_q_a[j].reshape(1, -1), wqn, wqp,
                          jnp.tile(mla_g_q_nope[j], N_HEADS).reshape(1, -1),
                          jnp.tile(mla_g_q_pe[j], N_HEADS).reshape(1, -1),
                          cos_t, sin_t, tiles["prep_tm"], s, scale)
            att = mla_attention(qcat.reshape(b, s, -1), kcat, vcat, tiles["mla_tq"])
            w_o = mla_w_o[j].astype(BF16)

        att = att.reshape(t, -1)
        m = i // 2
        if i % 2 == 0:
            h = dense_ffn(att, w_o, h, g_ffn[i], *ffn_w, m, tiles["ffn_tm"])
        else:
            h = moe_layer(att, w_o, h, g_ffn[i], moe_w_router[m], moe_b_router[m], *moe_w, m, tiles)
    return h.reshape(b, s, d)
```

```python
import functools
import math

import jax
import jax.numpy as jnp
from jax import lax
from jax.experimental import pallas as pl
from jax.experimental.pallas import tpu as pltpu

F32 = jnp.float32
BF16 = jnp.bfloat16
I32 = jnp.int32

EPS = 1e-6
N_HEADS = 16
SB_HEAD_DIM = 64
NOPE_DIM = 64
ROPE_DIM = 32
V_DIM = 64
KV_RANK = 256
ROPE_THETA = 10000.0
N_EXPERTS = 8
LANES = 128
PAIR_W = 2 * NOPE_DIM + LANES
ONES_LANE = 2 * NOPE_DIM + 2 * ROPE_DIM
MLA_STATIC_SHIFT_MAX = 50.0
MLA_GROUP = 4
LOG2E = 1.4426950408889634
SB_SKIP_EXP = 110.0
SB_SUBS = 4
SB_MASKED_LOGIT = -1e30
MXU_TILE = 256
VMEM_LIMIT = 52 * 1024 * 1024
VMEM_LIMIT_BIG = 58 * 1024 * 1024


def _params(sem, vmem=VMEM_LIMIT):
    return pltpu.CompilerParams(dimension_semantics=sem, vmem_limit_bytes=vmem)


def _dot(a, b):
    return jnp.dot(a, b, preferred_element_type=F32)


def _dot_nt(a, b):
    return lax.dot_general(a, b, (((1,), (1,)), ((), ())), preferred_element_type=F32)


def _split_bf16(x):
    hi = x.astype(BF16)
    lo = (x - hi.astype(F32)).astype(BF16)
    return hi, lo


def _dot_hilo(x, w):
    hi, lo = _split_bf16(x)
    return _dot(hi, w) + _dot(lo, w)


def _rms_matmul_body(x_ref, g_ref, w_ref, o_ref, xn_ref):
    @pl.when(pl.program_id(1) == 0)
    def _():
        x = x_ref[...]
        inv = lax.rsqrt(jnp.mean(x * x, axis=-1, keepdims=True) + EPS)
        xn_ref[...] = (x * inv * g_ref[...]).astype(BF16)

    o_ref[...] = _dot(xn_ref[...], w_ref[...]).astype(o_ref.dtype)


def rms_matmul(x, g, w, out_dtype, tm, tn):
    t, d = x.shape
    n = w.shape[1]
    w_mode = pl.Buffered(1) if tn == n else None
    return pl.pallas_call(
        _rms_matmul_body,
        grid=(t // tm, n // tn),
        in_specs=[pl.BlockSpec((tm, d), lambda i, j: (i, 0)),
                  pl.BlockSpec((1, d), lambda i, j: (0, 0)),
                  pl.BlockSpec((d, tn), lambda i, j: (0, j), pipeline_mode=w_mode)],
        out_specs=pl.BlockSpec((tm, tn), lambda i, j: (i, j)),
        out_shape=jax.ShapeDtypeStruct((t, n), out_dtype),
        scratch_shapes=[pltpu.VMEM((tm, d), BF16)],
        compiler_params=_params(("parallel", "arbitrary")),
        name="rms_matmul",
    )(x, g.reshape(1, d), w)


def _softplus2(z):
    return jnp.maximum(jnp.log2(1.0 + jnp.exp2(jnp.minimum(z, 64.0))), z)


def _key_norm_max(k_ref, head_masks, kmax_ref):
    k = k_ref[0].astype(F32)
    k2 = k * k
    for hh, mask in enumerate(head_masks):
        n2 = jnp.sum(jnp.where(mask, k2, 0.0), axis=-1, keepdims=True)
        kmax_ref[hh] = jnp.max(n2, axis=0, keepdims=True)


def _logit_bound(qm, kmax):
    qf = qm.astype(F32)
    return jnp.sqrt(jnp.sum(qf * qf, axis=-1, keepdims=True) * kmax) * 1.01 + 1e-2


def _sb_attn_body(q_ref, k_ref, v_ref, tri_ref, o_ref, kmax_ref, qs_ref, zb_ref, carry_ref, acc_ref,
                  *, tq):
    step = pl.program_id(2)
    lane = lax.broadcasted_iota(I32, (1, LANES), 1)
    head_masks = [(lane // SB_HEAD_DIM) == hh for hh in range(2)]

    @pl.when(step == 0)
    def _():
        _key_norm_max(k_ref, head_masks, kmax_ref)

    for sub in range(SB_SUBS):
        q2 = q_ref[0, sub * tq:(sub + 1) * tq, :]
        for hh in range(2):
            qm = jnp.where(head_masks[hh], q2, jnp.zeros_like(q2))
            qs_ref[2 * sub + hh] = qm
            zb_ref[2 * sub + hh] = _logit_bound(qm, kmax_ref[hh])

    row = lax.broadcasted_iota(I32, (tq, tq), 0)
    col = lax.broadcasted_iota(I32, (tq, tq), 1)
    causal = col < row

    def chain(c, start, masked):
        z = _dot_nt(qs_ref[c], k_ref[0, pl.ds(start, tq), :])
        if masked:
            z = jnp.where(causal, z, SB_MASKED_LOGIT)
        sp = _softplus2(z)
        csum = _dot(sp.astype(BF16), tri_ref[...])
        e = jnp.exp2(z - csum)
        return _dot(e.astype(BF16), v_ref[0, pl.ds(start, tq), :]), csum[:, 0:1]

    for sub in range(SB_SUBS):
        qi = SB_SUBS * step + sub
        has_prev = qi >= 1
        prev_start = pl.multiple_of(jnp.maximum(qi - 1, 0) * tq, tq)
        for hh in range(2):
            c = 2 * sub + hh
            pv_d, tot_d = chain(c, pl.multiple_of(qi * tq, tq), True)
            pv_p, tot_p = chain(c, prev_start, False)
            acc_ref[c] = pv_d + jnp.where(has_prev, jnp.exp2(-tot_d), 0.0) * pv_p
            carry_ref[c] = tot_d + jnp.where(has_prev, tot_p, 0.0)

    for sub in range(SB_SUBS):
        qi = SB_SUBS * step + sub
        chains = slice(2 * sub, 2 * sub + 2)

        def cond(jj):
            live = jnp.max(zb_ref[chains] - carry_ref[chains]) > -SB_SKIP_EXP * LOG2E
            return jnp.logical_and(jj <= qi, live)

        def body(jj):
            start = pl.multiple_of((qi - jj) * tq, tq)
            for c in range(2 * sub, 2 * sub + 2):
                pv, tot = chain(c, start, False)
                carry = carry_ref[c]
                acc_ref[c] += jnp.exp2(-carry) * pv
                carry_ref[c] = carry + tot
            return jj + 1

        lax.while_loop(cond, body, jnp.int32(2))
        o_ref[0, sub * tq:(sub + 1) * tq, :] = jnp.where(
            head_masks[0], acc_ref[2 * sub], acc_ref[2 * sub + 1]).astype(o_ref.dtype)


def sb_attention(qkv, tq):
    b, s, _ = qkv.shape
    n_pairs = N_HEADS // 2
    rows = SB_SUBS * tq
    n_chains = 2 * SB_SUBS
    ji = lax.broadcasted_iota(I32, (tq, tq), 0)
    si = lax.broadcasted_iota(I32, (tq, tq), 1)
    tri = (ji >= si).astype(BF16)
    return pl.pallas_call(
        functools.partial(_sb_attn_body, tq=tq),
        grid=(b, n_pairs, s // rows),
        in_specs=[pl.BlockSpec((1, rows, LANES), lambda bi, p, i: (bi, i, p)),
                  pl.BlockSpec((1, s, LANES), lambda bi, p, i: (bi, 0, n_pairs + p)),
                  pl.BlockSpec((1, s, LANES), lambda bi, p, i: (bi, 0, 2 * n_pairs + p)),
                  pl.BlockSpec((tq, tq), lambda bi, p, i: (0, 0))],
        out_specs=pl.BlockSpec((1, rows, LANES), lambda bi, p, i: (bi, i, p)),
        out_shape=jax.ShapeDtypeStruct((b, s, N_HEADS * SB_HEAD_DIM), BF16),
        scratch_shapes=[pltpu.VMEM((2, 1, 1), F32),
                        pltpu.VMEM((n_chains, tq, LANES), BF16),
                        pltpu.VMEM((n_chains, tq, 1), F32),
                        pltpu.VMEM((n_chains, tq, 1), F32),
                        pltpu.VMEM((n_chains, tq, LANES), F32)],
        compiler_params=_params(("parallel", "parallel", "arbitrary")),
        name="sb_attention",
    )(qkv, qkv, qkv, tri)


def _mla_attn_body(shift_ref, q_ref, k_ref, v_ref, o_ref, qs_ref, m_ref, acc_ref, *, tq):
    qi = pl.program_id(2)
    shift_max = shift_ref[0]
    lane_q = lax.broadcasted_iota(I32, (1, PAIR_W), 1)
    lane_o = lax.broadcasted_iota(I32, (1, LANES), 1)
    q2 = q_ref[0]
    for hh in range(2):
        nope = (lane_q // NOPE_DIM) == hh
        pe = jnp.logical_and(lane_q >= 2 * NOPE_DIM, ((lane_q - 2 * NOPE_DIM) // ROPE_DIM) == hh)
        keep = jnp.logical_or(jnp.logical_or(nope, pe), lane_q == ONES_LANE + hh)
        qs_ref[hh] = jnp.where(keep, q2, jnp.zeros_like(q2))
        acc_ref[hh] = jnp.zeros((tq, 2 * LANES), F32)

    ones = jnp.ones((tq, LANES), BF16)
    causal = (lax.broadcasted_iota(I32, (tq, tq), 1) <= lax.broadcasted_iota(I32, (tq, tq), 0))

    def scores(hh, j, masked):
        sc = _dot_nt(qs_ref[hh], k_ref[0, pl.ds(pl.multiple_of(j * tq, tq), tq), :])
        return jnp.where(causal, sc, -jnp.inf) if masked else sc

    def values(j):
        return jnp.concatenate([v_ref[0, pl.ds(pl.multiple_of(j * tq, tq), tq), :], ones], axis=-1)

    def static_blocks(blocks):
        for hh in range(2):
            acc_ref[hh] += sum(_dot(jnp.exp2(scores(hh, j, masked)).astype(BF16), values(j))
                               for j, masked in blocks)

    def online_block(j, masked):
        for hh in range(2):
            sc = scores(hh, j, masked)
            m_prev = m_ref[hh]
            m_new = jnp.maximum(m_prev, jnp.max(sc, axis=-1, keepdims=True))
            p = jnp.exp2(sc - m_new)
            acc_ref[hh] = jnp.exp2(m_prev - m_new) * acc_ref[hh] + _dot(p.astype(BF16), values(j))
            m_ref[hh] = m_new

    @pl.when(shift_max <= MLA_STATIC_SHIFT_MAX)
    def _():
        def group(j, c):
            static_blocks([(MLA_GROUP * j + u, False) for u in range(MLA_GROUP)])
            return c
        lax.fori_loop(0, qi // MLA_GROUP, group, 0)

        for rest in range(MLA_GROUP):
            @pl.when(qi % MLA_GROUP == rest)
            def _():
                static_blocks([(qi - rest + u, False) for u in range(rest)] + [(qi, True)])

    @pl.when(shift_max > MLA_STATIC_SHIFT_MAX)
    def _():
        m_ref[...] = jnp.full_like(m_ref, -jnp.inf)

        def body(j, c):
            online_block(j, False)
            return c
        lax.fori_loop(0, qi, body, 0)
        online_block(qi, True)

    outs = [acc_ref[hh][:, :LANES] / acc_ref[hh][:, LANES:] for hh in range(2)]
    o_ref[0] = jnp.where(lane_o < V_DIM, outs[0], outs[1]).astype(o_ref.dtype)


def mla_attention(shift, qcat, kcat, v, tq):
    b, s, _ = qcat.shape
    n_pairs = N_HEADS // 2
    grid_spec = pltpu.PrefetchScalarGridSpec(
        num_scalar_prefetch=1,
        grid=(b, n_pairs, s // tq),
        in_specs=[pl.BlockSpec((1, tq, PAIR_W), lambda bi, p, i, sh: (bi, i, p)),
                  pl.BlockSpec((1, s, PAIR_W), lambda bi, p, i, sh: (bi, 0, p)),
                  pl.BlockSpec((1, s, LANES), lambda bi, p, i, sh: (bi, 0, p))],
        out_specs=pl.BlockSpec((1, tq, LANES), lambda bi, p, i, sh: (bi, i, p)),
        scratch_shapes=[pltpu.VMEM((2, tq, PAIR_W), BF16),
                        pltpu.VMEM((2, tq, 1), F32),
                        pltpu.VMEM((2, tq, 2 * LANES), F32)],
    )
    return pl.pallas_call(
        functools.partial(_mla_attn_body, tq=tq),
        grid_spec=grid_spec,
        out_shape=jax.ShapeDtypeStruct((b, s, N_HEADS * V_DIM), BF16),
        compiler_params=_params(("parallel", "parallel", "arbitrary")),
        name="mla_attention",
    )(shift, qcat, kcat, v)


def _group_rms(x, gmat, group, g):
    chunks = []
    for c in range(x.shape[1] // 256):
        xc = x[:, c * 256:(c + 1) * 256]
        ss = _dot_hilo(xc * xc, gmat)
        chunks.append(xc * lax.rsqrt(ss * (1.0 / group) + EPS))
    y = chunks[0] if len(chunks) == 1 else jnp.concatenate(chunks, axis=-1)
    return y * g


def _rope128(x, cos, sin):
    lane = lax.broadcasted_iota(I32, (1, LANES), 1)
    first_half = (lane % ROPE_DIM) < (ROPE_DIM // 2)
    rot = jnp.where(first_half,
                    -pltpu.roll(x, LANES - ROPE_DIM // 2, 1),
                    pltpu.roll(x, ROPE_DIM // 2, 1))
    return x * cos + rot * sin


def _kv_prep_body(kva_ref, g_a_ref, g_pe_ref, cos_ref, sin_ref, wn_ref, wv_ref, g_n_ref,
                  g64_ref, k_ref, v_ref):
    lane = lax.broadcasted_iota(I32, (1, LANES), 1)
    kva = kva_ref[...]
    c = kva[:, :KV_RANK]
    cn = (c * lax.rsqrt(jnp.mean(c * c, axis=-1, keepdims=True) + EPS) * g_a_ref[...]).astype(BF16)
    pe = kva[:, KV_RANK:KV_RANK + LANES]
    ms = jnp.sum(pe * pe, axis=-1, keepdims=True) * (1.0 / ROPE_DIM)
    pen = pe * lax.rsqrt(ms + EPS) * g_pe_ref[...]
    kpe = _rope128(pen, cos_ref[...], sin_ref[...])
    kpe = jnp.where(lane < ROPE_DIM, kpe, 0.0)
    kpe2 = kpe + pltpu.roll(kpe, ROPE_DIM, 1)
    ones_lanes = jnp.logical_or(lane == ONES_LANE - LANES, lane == ONES_LANE + 1 - LANES)
    kpe2 = jnp.where(ones_lanes, 1.0, kpe2).astype(BF16)
    kn = _group_rms(_dot(cn, wn_ref[...]), g64_ref[...], NOPE_DIM, g_n_ref[...]).astype(BF16)
    v_ref[...] = _dot(cn, wv_ref[...]).astype(BF16)
    for p in range(N_HEADS // 2):
        k_ref[:, p * PAIR_W:p * PAIR_W + LANES] = kn[:, p * LANES:(p + 1) * LANES]
        k_ref[:, p * PAIR_W + LANES:(p + 1) * PAIR_W] = kpe2


def _q_prep_body(qa_ref, g_a_ref, wn_ref, wp_ref, g_n_ref, g_p_ref, cos_ref, sin_ref,
                 g64_ref, g32_ref, shift_ref, q_ref, *, scale):
    lane = lax.broadcasted_iota(I32, (1, LANES), 1)
    shift_lanes = jnp.logical_or(lane == ONES_LANE - LANES, lane == ONES_LANE + 1 - LANES)
    neg_shift = -shift_ref[...]
    qa = qa_ref[...]
    qan = (qa * lax.rsqrt(jnp.mean(qa * qa, axis=-1, keepdims=True) + EPS) * g_a_ref[...]).astype(BF16)
    qn = _group_rms(_dot(qan, wn_ref[...]), g64_ref[...], NOPE_DIM, g_n_ref[...]) * scale
    qp = _group_rms(_dot(qan, wp_ref[...]), g32_ref[...], ROPE_DIM, g_p_ref[...])
    cos = cos_ref[...]
    sin = sin_ref[...]
    for c in range(N_HEADS * ROPE_DIM // LANES):
        pe4 = _rope128(qp[:, c * LANES:(c + 1) * LANES], cos, sin) * scale
        for half in range(2):
            p = 2 * c + half
            x = pe4 if half == 0 else pltpu.roll(pe4, 2 * ROPE_DIM, 1)
            q_ref[:, p * PAIR_W:p * PAIR_W + LANES] = qn[:, p * LANES:(p + 1) * LANES].astype(BF16)
            x = jnp.where(lane < 2 * ROPE_DIM, x, jnp.where(shift_lanes, neg_shift, 0.0))
            q_ref[:, p * PAIR_W + LANES:(p + 1) * PAIR_W] = x.astype(BF16)


def _block_diag_ones(n, group):
    a = lax.broadcasted_iota(I32, (n, n), 0) // group
    b = lax.broadcasted_iota(I32, (n, n), 1) // group
    return (a == b).astype(BF16)


def _full(shape):
    return pl.BlockSpec(shape, lambda i: (0,) * len(shape))


def kv_prep(kva, g_a, g_pe_pad, cos_t, sin_t, wn, wv, g_n_t, tm, seq):
    t = kva.shape[0]
    nseq = seq // tm
    return pl.pallas_call(
        _kv_prep_body,
        grid=(t // tm,),
        in_specs=[pl.BlockSpec((tm, kva.shape[1]), lambda i: (i, 0)),
                  _full((1, KV_RANK)), _full((1, LANES)),
                  pl.BlockSpec((tm, LANES), lambda i: (i % nseq, 0)),
                  pl.BlockSpec((tm, LANES), lambda i: (i % nseq, 0)),
                  _full(wn.shape), _full(wv.shape), _full((1, wn.shape[1])),
                  _full((256, 256))],
        out_specs=[pl.BlockSpec((tm, (N_HEADS // 2) * PAIR_W), lambda i: (i, 0)),
                   pl.BlockSpec((tm, N_HEADS * V_DIM), lambda i: (i, 0))],
        out_shape=[jax.ShapeDtypeStruct((t, (N_HEADS // 2) * PAIR_W), BF16),
                   jax.ShapeDtypeStruct((t, N_HEADS * V_DIM), BF16)],
        compiler_params=_params(("parallel",)),
        name="kv_prep",
    )(kva, g_a, g_pe_pad, cos_t, sin_t, wn, wv, g_n_t, _block_diag_ones(256, NOPE_DIM))


def q_prep(qa, g_a, wn, wp, g_n_t, g_p_t, cos_t, sin_t, shift, tm, seq, scale):
    t = qa.shape[0]
    nseq = seq // tm
    return pl.pallas_call(
        functools.partial(_q_prep_body, scale=scale),
        grid=(t // tm,),
        in_specs=[pl.BlockSpec((tm, qa.shape[1]), lambda i: (i, 0)),
                  _full((1, qa.shape[1])), _full(wn.shape), _full(wp.shape),
                  _full((1, wn.shape[1])), _full((1, wp.shape[1])),
                  pl.BlockSpec((tm, LANES), lambda i: (i % nseq, 0)),
                  pl.BlockSpec((tm, LANES), lambda i: (i % nseq, 0)),
                  _full((256, 256)), _full((256, 256)), _full((1, 1))],
        out_specs=pl.BlockSpec((tm, (N_HEADS // 2) * PAIR_W), lambda i: (i, 0)),
        out_shape=jax.ShapeDtypeStruct((t, (N_HEADS // 2) * PAIR_W), BF16),
        compiler_params=_params(("parallel",)),
        name="q_prep",
    )(qa, g_a, wn, wp, g_n_t, g_p_t, cos_t, sin_t,
      _block_diag_ones(256, NOPE_DIM), _block_diag_ones(256, ROPE_DIM), shift.reshape(1, 1))


def _ffn_body(att_ref, wo_ref, res_ref, g_ref, wg_ref, wu_ref, wd_ref, o_ref, h_ref):
    x = res_ref[...] + _dot(att_ref[...], wo_ref[...])
    inv = lax.rsqrt(jnp.mean(x * x, axis=-1, keepdims=True) + EPS)
    xn = (x * inv * g_ref[...]).astype(BF16)
    _swiglu_hidden(xn, wg_ref, wu_ref, h_ref, ())
    o_ref[...] = x + _dot(h_ref[...], wd_ref[...])


def dense_ffn(att, wo, res, g, wg, wu, wd, layer, tm):
    t, d = res.shape
    f = wg.shape[2]
    once = pl.Buffered(1)
    return pl.pallas_call(
        _ffn_body,
        grid=(t // tm,),
        in_specs=[pl.BlockSpec((tm, att.shape[1]), lambda i: (i, 0)),
                  pl.BlockSpec(wo.shape, lambda i: (0, 0), pipeline_mode=once),
                  pl.BlockSpec((tm, d), lambda i: (i, 0)),
                  pl.BlockSpec((1, d), lambda i: (0, 0)),
                  pl.BlockSpec((None, d, f), lambda i: (layer, 0, 0), pipeline_mode=once),
                  pl.BlockSpec((None, d, f), lambda i: (layer, 0, 0), pipeline_mode=once),
                  pl.BlockSpec((None, f, d), lambda i: (layer, 0, 0), pipeline_mode=once)],
        out_specs=pl.BlockSpec((tm, d), lambda i: (i, 0)),
        out_shape=jax.ShapeDtypeStruct((t, d), F32),
        scratch_shapes=[pltpu.VMEM((tm, f), BF16)],
        compiler_params=_params(("parallel",)),
        name="dense_ffn",
    )(att, wo, res, g.reshape(1, d), wg, wu, wd)


def _router_body(att_ref, wo_ref, res_ref, g_ref, wr_ref, b_ref, upper_ref,
                 x_ref, xn_ref, idx_ref, gate_ref, rank_ref, cnt_ref, base_ref):
    i = pl.program_id(0)

    @pl.when(i == 0)
    def _():
        base_ref[...] = jnp.zeros_like(base_ref)

    x = res_ref[...] + _dot(att_ref[...], wo_ref[...])
    x_ref[...] = x
    xn = x * lax.rsqrt(jnp.mean(x * x, axis=-1, keepdims=True) + EPS) * g_ref[...]
    xn_ref[...] = xn
    tm = x.shape[0]

    x_hi, x_lo = _split_bf16(xn)
    w_hi, w_lo = _split_bf16(wr_ref[...])
    logits = _dot_nt(w_hi, x_hi) + _dot_nt(w_hi, x_lo) + _dot_nt(w_lo, x_hi) + b_ref[...]

    e_iota = lax.broadcasted_iota(I32, (N_EXPERTS, tm), 0)
    m1 = jnp.max(logits, axis=0, keepdims=True)
    i1 = jnp.min(jnp.where(logits == m1, e_iota, N_EXPERTS), axis=0, keepdims=True)
    sel1 = e_iota == i1
    rest = jnp.where(sel1, -jnp.inf, logits)
    m2 = jnp.max(rest, axis=0, keepdims=True)
    i2 = jnp.min(jnp.where(rest == m2, e_iota, N_EXPERTS), axis=0, keepdims=True)
    sel2 = e_iota == i2
    e2 = jnp.exp(m2 - m1)
    g1 = 1.0 / (1.0 + e2)
    idx_ref[...] = jnp.concatenate([i1, i2], axis=0)
    gate_ref[...] = jnp.concatenate([g1, e2 * g1], axis=0)

    member = jnp.logical_or(sel1, sel2)
    prefix = _dot(member.astype(BF16), upper_ref[...])
    rank = prefix + base_ref[...]
    r1 = jnp.sum(jnp.where(sel1, rank, 0.0), axis=0, keepdims=True)
    r2 = jnp.sum(jnp.where(sel2, rank, 0.0), axis=0, keepdims=True)
    rank_ref[...] = jnp.concatenate([r1, r2], axis=0).astype(I32)
    base_ref[...] += jnp.sum(member.astype(F32), axis=1, keepdims=True)
    cnt_ref[...] = jnp.broadcast_to(base_ref[...], cnt_ref.shape).astype(I32)


def moe_router(att, wo, res, g, w_router, b_router, tm):
    t, d = res.shape
    a = lax.broadcasted_iota(I32, (tm, tm), 0)
    b = lax.broadcasted_iota(I32, (tm, tm), 1)
    upper = (a < b).astype(BF16)
    return pl.pallas_call(
        _router_body,
        grid=(t // tm,),
        in_specs=[pl.BlockSpec((tm, att.shape[1]), lambda i: (i, 0)), _full(wo.shape),
                  pl.BlockSpec((tm, d), lambda i: (i, 0)),
                  _full((1, d)), _full((N_EXPERTS, d)), _full((N_EXPERTS, 1)), _full((tm, tm))],
        out_specs=[pl.BlockSpec((tm, d), lambda i: (i, 0)),
                   pl.BlockSpec((tm, d), lambda i: (i, 0)),
                   pl.BlockSpec((2, tm), lambda i: (0, i)),
                   pl.BlockSpec((2, tm), lambda i: (0, i)),
                   pl.BlockSpec((2, tm), lambda i: (0, i)),
                   _full((N_EXPERTS, LANES))],
        out_shape=[jax.ShapeDtypeStruct((t, d), F32),
                   jax.ShapeDtypeStruct((t, d), F32),
                   jax.ShapeDtypeStruct((2, t), I32),
                   jax.ShapeDtypeStruct((2, t), F32),
                   jax.ShapeDtypeStruct((2, t), I32),
                   jax.ShapeDtypeStruct((N_EXPERTS, LANES), I32)],
        scratch_shapes=[pltpu.VMEM((N_EXPERTS, 1), F32)],
        compiler_params=_params(("arbitrary",)),
        name="moe_router",
    )(att, wo, res, g.reshape(1, d), w_router.T, b_router.reshape(N_EXPERTS, 1), upper)


def _row_copy(src, src_row, dst, dst_row, sem):
    return pltpu.make_async_copy(src.at[pl.ds(src_row, 1)], dst.at[pl.ds(dst_row, 1)], sem)


def _dispatch_body(dest_ref, xn_ref, buf_in_hbm, buf_hbm, sem, *, tb, t_total):
    del buf_in_hbm
    base = pl.program_id(0) * tb

    def issue(t, c):
        for k in range(2):
            _row_copy(xn_ref, t, buf_hbm, dest_ref[k * t_total + base + t], sem).start()
        return c

    lax.fori_loop(0, tb, issue, 0, unroll=8)

    def drain(t, c):
        for k in range(2):
            _row_copy(xn_ref, 0, buf_hbm, 0, sem).wait()
        return c

    lax.fori_loop(0, tb, drain, 0, unroll=8)


def moe_dispatch(dest_flat, xn, n_rows, tb):
    t, d = xn.shape
    buf0 = jnp.zeros((n_rows, d), xn.dtype)
    grid_spec = pltpu.PrefetchScalarGridSpec(
        num_scalar_prefetch=1,
        grid=(t // tb,),
        in_specs=[pl.BlockSpec((tb, d), lambda i, dest: (i, 0)), pl.BlockSpec(memory_space=pl.ANY)],
        out_specs=pl.BlockSpec(memory_space=pl.ANY),
        scratch_shapes=[pltpu.SemaphoreType.DMA(())],
    )
    return pl.pallas_call(
        functools.partial(_dispatch_body, tb=tb, t_total=t),
        grid_spec=grid_spec,
        out_shape=jax.ShapeDtypeStruct((n_rows, d), xn.dtype),
        input_output_aliases={2: 0},
        compiler_params=_params(("arbitrary",)),
        name="moe_dispatch",
    )(dest_flat, xn, buf0)


def _swiglu_hidden(xb, wg_ref, wu_ref, h_ref, widx):
    f = h_ref.shape[1]
    for c0 in range(0, f, MXU_TILE):
        c1 = min(c0 + MXU_TILE, f)
        gate = _dot(xb, wg_ref[widx + (slice(None), slice(c0, c1))])
        up = _dot(xb, wu_ref[widx + (slice(None), slice(c0, c1))])
        h_ref[:, c0:c1] = (gate * jax.nn.sigmoid(gate) * up).astype(BF16)


def _experts_body(be_ref, nv_ref, x_ref, wg_ref, wu_ref, wd_ref, o_ref, h_ref):
    del be_ref
    valid = pl.program_id(0) < nv_ref[0]

    @pl.when(valid)
    def _():
        _swiglu_hidden(x_ref[...].astype(BF16), wg_ref, wu_ref, h_ref, (0,))
        o_ref[...] = _dot(h_ref[...], wd_ref[0])

    @pl.when(jnp.logical_not(valid))
    def _():
        o_ref[...] = jnp.zeros_like(o_ref)


def moe_experts(block_e, n_valid, buf, wg, wu, wd, layer, tm):
    n_rows, d = buf.shape
    f = wg.shape[3]
    once = pl.Buffered(1)
    grid_spec = pltpu.PrefetchScalarGridSpec(
        num_scalar_prefetch=2,
        grid=(n_rows // tm,),
        in_specs=[pl.BlockSpec((tm, d), lambda i, be, nv: (i, 0)),
                  pl.BlockSpec((None, 1, d, f), lambda i, be, nv: (layer, be[i], 0, 0), pipeline_mode=once),
                  pl.BlockSpec((None, 1, d, f), lambda i, be, nv: (layer, be[i], 0, 0), pipeline_mode=once),
                  pl.BlockSpec((None, 1, f, d), lambda i, be, nv: (layer, be[i], 0, 0), pipeline_mode=once)],
        out_specs=pl.BlockSpec((tm, d), lambda i, be, nv: (i, 0)),
        scratch_shapes=[pltpu.VMEM((tm, f), BF16)],
    )
    return pl.pallas_call(
        _experts_body,
        grid_spec=grid_spec,
        out_shape=jax.ShapeDtypeStruct((n_rows, d), F32),
        compiler_params=_params(("arbitrary",), vmem=VMEM_LIMIT_BIG),
        name="moe_experts",
    )(block_e, n_valid, buf, wg, wu, wd)


def _combine_body(dest_ref, x_ref, gate_ref, eo_hbm, o_ref, buf_ref, sems, *, tb, t_total):
    i = pl.program_id(0)
    slot = i % 2

    def start_block(block, to_slot):
        base = block * tb

        def issue(t, c):
            for k in range(2):
                _row_copy(eo_hbm, dest_ref[k * t_total + base + t], buf_ref.at[to_slot, k], t,
                          sems.at[to_slot]).start()
            return c

        lax.fori_loop(0, tb, issue, 0, unroll=8)

    @pl.when(i == 0)
    def _():
        start_block(0, 0)

    @pl.when(i + 1 < pl.num_programs(0))
    def _():
        start_block(i + 1, 1 - slot)

    def drain(t, c):
        for k in range(2):
            _row_copy(eo_hbm, 0, buf_ref.at[slot, k], 0, sems.at[slot]).wait()
        return c

    lax.fori_loop(0, tb, drain, 0, unroll=8)
    gates = gate_ref[...]
    o_ref[...] = x_ref[...] + gates[:, 0:1] * buf_ref[slot, 0] + gates[:, 1:2] * buf_ref[slot, 1]


def moe_combine(dest_flat, x, gates_t, expert_out, tb):
    t, d = x.shape
    grid_spec = pltpu.PrefetchScalarGridSpec(
        num_scalar_prefetch=1,
        grid=(t // tb,),
        in_specs=[pl.BlockSpec((tb, d), lambda i, dest: (i, 0)),
                  pl.BlockSpec((tb, 2), lambda i, dest: (i, 0)),
                  pl.BlockSpec(memory_space=pl.ANY)],
        out_specs=pl.BlockSpec((tb, d), lambda i, dest: (i, 0)),
        scratch_shapes=[pltpu.VMEM((2, 2, tb, d), F32), pltpu.SemaphoreType.DMA((2,))],
    )
    return pl.pallas_call(
        functools.partial(_combine_body, tb=tb, t_total=t),
        grid_spec=grid_spec,
        out_shape=jax.ShapeDtypeStruct((t, d), F32),
        compiler_params=_params(("arbitrary",)),
        name="moe_combine",
    )(dest_flat, x, gates_t, expert_out)


def moe_layer(att, wo, res, g, w_router, b_router, wg, wu, wd, layer, tiles):
    t, d = res.shape
    tm = tiles["moe_tm"]
    x, xn, idx, gates, rank, cnt = moe_router(att, wo, res, g, w_router, b_router, tiles["router_tm"])
    counts = cnt[:, 0]
    padded = ((counts + tm - 1) // tm) * tm
    pends = jnp.cumsum(padded)
    pstarts = pends - padded
    group_start = sum(jnp.where(idx == e, pstarts[e], 0) for e in range(N_EXPERTS))
    dest_flat = (group_start + rank).reshape(-1)
    n_rows = 2 * t + N_EXPERTS * tm
    n_blocks = n_rows // tm
    block_row = jnp.arange(n_blocks, dtype=I32) * tm
    block_e = jnp.minimum(sum((block_row >= pends[e]).astype(I32) for e in range(N_EXPERTS)),
                          N_EXPERTS - 1)
    n_valid = (pends[-1:] // tm).astype(I32)
    buf = moe_dispatch(dest_flat, xn, n_rows, tiles["moe_tb"])
    eo = moe_experts(block_e, n_valid, buf, wg, wu, wd, layer, tm)
    return moe_combine(dest_flat, x, gates.T, eo, tiles["moe_tb"])


def _tiles(t, s):
    def fit(n, want):
        while n % want:
            want //= 2
        return want
    return dict(
        proj_tm=fit(t, 1024), attn_tq=fit(s, 256), mla_tq=fit(s, 512), prep_tm=fit(s, 512),
        ffn_tm=fit(t, 512), router_tm=fit(t, 512), moe_tm=fit(t, 512), moe_tb=fit(t, 256),
    )


def _mla_score_bound(g_q_nope, g_q_pe, g_k_nope, g_k_pe, scale):
    q2 = NOPE_DIM * jnp.max(g_q_nope * g_q_nope) + ROPE_DIM * jnp.max(g_q_pe * g_q_pe)
    k2 = NOPE_DIM * jnp.max(g_k_nope * g_k_nope) + ROPE_DIM * jnp.max(g_k_pe * g_k_pe)
    return (scale * jnp.sqrt(q2 * k2) * 1.02 + 1e-2).reshape(1)


def _rope_tables(seq):
    pos = jnp.arange(seq, dtype=F32)
    inv_freq = ROPE_THETA ** (-jnp.arange(0, ROPE_DIM, 2, dtype=F32) / ROPE_DIM)
    ang = pos[:, None] * inv_freq[None, :]
    ang = jnp.concatenate([ang, ang], axis=-1)
    reps = LANES // ROPE_DIM
    return jnp.tile(jnp.cos(ang), (1, reps)), jnp.tile(jnp.sin(ang), (1, reps))


def kernel(x, g_mix, g_ffn, sb_w_qkv, sb_w_o, kv_g_src, kv_w_a, kv_g_a, kv_w_b, kv_g_k_nope, kv_g_k_pe, mla_w_q_a, mla_g_q_a, mla_w_q_b, mla_g_q_nope, mla_g_q_pe, mla_w_o, ffn_w_gate, ffn_w_up, ffn_w_down, moe_w_router, moe_b_router, moe_w_gate, moe_w_up, moe_w_down):
    b, s, d = x.shape
    t = b * s
    depth = g_mix.shape[0]
    n_a = sb_w_qkv.shape[0]
    tiles = _tiles(t, s)
    cos_t, sin_t = _rope_tables(s)
    h = x.reshape(t, d)
    hw = N_HEADS * SB_HEAD_DIM
    kcat = vcat = None
    ffn_w = [w.astype(BF16) for w in (ffn_w_gate, ffn_w_up, ffn_w_down)]
    moe_w = [w.astype(BF16) for w in (moe_w_gate, moe_w_up, moe_w_down)]

    for i in range(depth):
        if i < n_a:
            col_scale = jnp.concatenate([jnp.full((hw,), LOG2E / math.sqrt(SB_HEAD_DIM), F32),
                                         jnp.ones((2 * hw,), F32)])
            w_qkv = (sb_w_qkv[i] * col_scale).astype(BF16)
            qkv = rms_matmul(h, g_mix[i], w_qkv, BF16, tiles["ffn_tm"], w_qkv.shape[1])
            att = sb_attention(qkv.reshape(b, s, 3 * hw), tiles["attn_tq"])
            w_o = sb_w_o[i].astype(BF16)
        else:
            j = i - n_a
            if kcat is None:
                pad = jnp.zeros((d, LANES - ROPE_DIM), F32)
                w_a = jnp.concatenate([kv_w_a, pad], axis=1).astype(BF16)
                kva = rms_matmul(h, kv_g_src, w_a, F32, tiles["proj_tm"], w_a.shape[1])
                w_b = kv_w_b.reshape(KV_RANK, N_HEADS, NOPE_DIM + V_DIM)
                wn = w_b[:, :, :NOPE_DIM].reshape(KV_RANK, N_HEADS * NOPE_DIM).astype(BF16)
                wv = w_b[:, :, NOPE_DIM:].reshape(KV_RANK, N_HEADS * V_DIM).astype(BF16)
                g_pe_pad = jnp.concatenate([kv_g_k_pe, jnp.zeros((LANES - ROPE_DIM,), F32)]).reshape(1, LANES)
                kcat, vcat = kv_prep(kva, kv_g_a.reshape(1, KV_RANK), g_pe_pad, cos_t, sin_t, wn, wv,
                                     jnp.tile(kv_g_k_nope, N_HEADS).reshape(1, -1), tiles["prep_tm"], s)
                kcat = kcat.reshape(b, s, -1)
                vcat = vcat.reshape(b, s, -1)
            qa = rms_matmul(h, g_mix[i], mla_w_q_a[j].astype(BF16), F32, tiles["proj_tm"], mla_w_q_a.shape[2])
            w_qb = mla_w_q_b[j].reshape(-1, N_HEADS, NOPE_DIM + ROPE_DIM)
            wqn = w_qb[:, :, :NOPE_DIM].reshape(-1, N_HEADS * NOPE_DIM).astype(BF16)
            wqp = w_qb[:, :, NOPE_DIM:].reshape(-1, N_HEADS * ROPE_DIM).astype(BF16)
            scale = LOG2E / math.sqrt(NOPE_DIM + ROPE_DIM)
            shift = _mla_score_bound(mla_g_q_nope[j], mla_g_q_pe[j], kv_g_k_nope, kv_g_k_pe, scale)
            qcat = q_prep(qa, mla_g_q_a[j].reshape(1, -1), wqn, wqp,
                          jnp.tile(mla_g_q_nope[j], N_HEADS).reshape(1, -1),
                          jnp.tile(mla_g_q_pe[j], N_HEADS).reshape(1, -1),
                          cos_t, sin_t, shift, tiles["prep_tm"], s, scale)
            att = mla_attention(shift, qcat.reshape(b, s, -1), kcat, vcat, tiles["mla_tq"])
            w_o = mla_w_o[j].astype(BF16)

        att = att.reshape(t, -1)
        m = i // 2
        if i % 2 == 0:
            h = dense_ffn(att, w_o, h, g_ffn[i], *ffn_w, m, tiles["ffn_tm"])
        else:
            h = moe_layer(att, w_o, h, g_ffn[i], moe_w_router[m], moe_b_router[m], *moe_w, m, tiles)
    return h.reshape(b, s, d)
```

```python
import functools
import math

import jax
import jax.numpy as jnp
from jax import lax
from jax.experimental import pallas as pl
from jax.experimental.pallas import tpu as pltpu

F32 = jnp.float32
BF16 = jnp.bfloat16
I32 = jnp.int32

EPS = 1e-6
N_HEADS = 16
SB_HEAD_DIM = 64
NOPE_DIM = 64
ROPE_DIM = 32
V_DIM = 64
KV_RANK = 256
ROPE_THETA = 10000.0
N_EXPERTS = 8
LANES = 128
PAIR_W = 2 * NOPE_DIM + LANES
ONES_LANE = 2 * NOPE_DIM + 2 * ROPE_DIM
MLA_STATIC_SHIFT_MAX = 50.0
MLA_GROUP = 4
LOG2E = 1.4426950408889634
SB_SKIP_EXP = 110.0
SB_SUBS = 8
SB_MASKED_LOGIT = -1e30
MXU_TILE = 256
VMEM_LIMIT = 52 * 1024 * 1024
VMEM_LIMIT_BIG = 58 * 1024 * 1024


def _params(sem, vmem=VMEM_LIMIT):
    return pltpu.CompilerParams(dimension_semantics=sem, vmem_limit_bytes=vmem)


def _dot(a, b):
    return jnp.dot(a, b, preferred_element_type=F32)


def _dot_nt(a, b):
    return lax.dot_general(a, b, (((1,), (1,)), ((), ())), preferred_element_type=F32)


def _split_bf16(x):
    hi = x.astype(BF16)
    lo = (x - hi.astype(F32)).astype(BF16)
    return hi, lo


def _dot_hilo(x, w):
    hi, lo = _split_bf16(x)
    return _dot(hi, w) + _dot(lo, w)


def _rms_matmul_body(x_ref, g_ref, w_ref, o_ref, xn_ref):
    @pl.when(pl.program_id(1) == 0)
    def _():
        x = x_ref[...]
        inv = lax.rsqrt(jnp.mean(x * x, axis=-1, keepdims=True) + EPS)
        xn_ref[...] = (x * inv * g_ref[...]).astype(BF16)

    o_ref[...] = _dot(xn_ref[...], w_ref[...]).astype(o_ref.dtype)


def rms_matmul(x, g, w, out_dtype, tm, tn):
    t, d = x.shape
    n = w.shape[1]
    w_mode = pl.Buffered(1) if tn == n else None
    return pl.pallas_call(
        _rms_matmul_body,
        grid=(t // tm, n // tn),
        in_specs=[pl.BlockSpec((tm, d), lambda i, j: (i, 0)),
                  pl.BlockSpec((1, d), lambda i, j: (0, 0)),
                  pl.BlockSpec((d, tn), lambda i, j: (0, j), pipeline_mode=w_mode)],
        out_specs=pl.BlockSpec((tm, tn), lambda i, j: (i, j)),
        out_shape=jax.ShapeDtypeStruct((t, n), out_dtype),
        scratch_shapes=[pltpu.VMEM((tm, d), BF16)],
        compiler_params=_params(("parallel", "arbitrary")),
        name="rms_matmul",
    )(x, g.reshape(1, d), w)


def _softplus2(z):
    return jnp.maximum(jnp.log2(1.0 + jnp.exp2(jnp.minimum(z, 64.0))), z)


def _key_norm_max(k_ref, head_masks, kmax_ref):
    k = k_ref[0].astype(F32)
    k2 = k * k
    for hh, mask in enumerate(head_masks):
        n2 = jnp.sum(jnp.where(mask, k2, 0.0), axis=-1, keepdims=True)
        kmax_ref[hh] = jnp.max(n2, axis=0, keepdims=True)


def _logit_bound(qm, kmax):
    qf = qm.astype(F32)
    return jnp.sqrt(jnp.sum(qf * qf, axis=-1, keepdims=True) * kmax) * 1.01 + 1e-2


def _sb_attn_body(q_ref, k_ref, v_ref, tri_ref, o_ref, kmax_ref, qs_ref, zb_ref, carry_ref, acc_ref,
                  *, tq):
    step = pl.program_id(2)
    lane = lax.broadcasted_iota(I32, (1, LANES), 1)
    head_masks = [(lane // SB_HEAD_DIM) == hh for hh in range(2)]

    @pl.when(step == 0)
    def _():
        _key_norm_max(k_ref, head_masks, kmax_ref)

    for sub in range(SB_SUBS):
        q2 = q_ref[0, sub * tq:(sub + 1) * tq, :]
        for hh in range(2):
            qm = jnp.where(head_masks[hh], q2, jnp.zeros_like(q2))
            qs_ref[sub, hh * tq:(hh + 1) * tq, :] = qm
            zb_ref[sub, hh * tq:(hh + 1) * tq, :] = _logit_bound(qm, kmax_ref[hh])

    row = lax.broadcasted_iota(I32, (2 * tq, tq), 0) % tq
    col = lax.broadcasted_iota(I32, (2 * tq, tq), 1)
    causal = col < row

    def chain(c, start, masked):
        z = _dot_nt(qs_ref[c], k_ref[0, pl.ds(start, tq), :])
        if masked:
            z = jnp.where(causal, z, SB_MASKED_LOGIT)
        sp = _softplus2(z)
        csum = _dot(sp.astype(BF16), tri_ref[...])
        e = jnp.exp2(z - csum)
        return _dot(e.astype(BF16), v_ref[0, pl.ds(start, tq), :]), csum[:, 0:1]

    for sub in range(SB_SUBS):
        qi = SB_SUBS * step + sub
        has_prev = qi >= 1
        prev_start = pl.multiple_of(jnp.maximum(qi - 1, 0) * tq, tq)
        pv_d, tot_d = chain(sub, pl.multiple_of(qi * tq, tq), True)
        pv_p, tot_p = chain(sub, prev_start, False)
        acc_ref[sub] = pv_d + jnp.where(has_prev, jnp.exp2(-tot_d), 0.0) * pv_p
        carry_ref[sub] = tot_d + jnp.where(has_prev, tot_p, 0.0)

    for sub in range(SB_SUBS):
        qi = SB_SUBS * step + sub

        def cond(jj):
            live = jnp.max(zb_ref[sub] - carry_ref[sub]) > -SB_SKIP_EXP * LOG2E
            return jnp.logical_and(jj <= qi, live)

        def body(jj):
            pv, tot = chain(sub, pl.multiple_of((qi - jj) * tq, tq), False)
            carry = carry_ref[sub]
            acc_ref[sub] += jnp.exp2(-carry) * pv
            carry_ref[sub] = carry + tot
            return jj + 1

        lax.while_loop(cond, body, jnp.int32(2))
        acc = acc_ref[sub]
        o_ref[0, sub * tq:(sub + 1) * tq, :] = jnp.where(
            head_masks[0], acc[:tq], acc[tq:]).astype(o_ref.dtype)


def sb_attention(qkv, tq):
    b, s, _ = qkv.shape
    n_pairs = N_HEADS // 2
    rows = SB_SUBS * tq
    ji = lax.broadcasted_iota(I32, (tq, tq), 0)
    si = lax.broadcasted_iota(I32, (tq, tq), 1)
    tri = (ji >= si).astype(BF16)
    return pl.pallas_call(
        functools.partial(_sb_attn_body, tq=tq),
        grid=(b, n_pairs, s // rows),
        in_specs=[pl.BlockSpec((1, rows, LANES), lambda bi, p, i: (bi, i, p)),
                  pl.BlockSpec((1, s, LANES), lambda bi, p, i: (bi, 0, n_pairs + p)),
                  pl.BlockSpec((1, s, LANES), lambda bi, p, i: (bi, 0, 2 * n_pairs + p)),
                  pl.BlockSpec((tq, tq), lambda bi, p, i: (0, 0))],
        out_specs=pl.BlockSpec((1, rows, LANES), lambda bi, p, i: (bi, i, p)),
        out_shape=jax.ShapeDtypeStruct((b, s, N_HEADS * SB_HEAD_DIM), BF16),
        scratch_shapes=[pltpu.VMEM((2, 1, 1), F32),
                        pltpu.VMEM((SB_SUBS, 2 * tq, LANES), BF16),
                        pltpu.VMEM((SB_SUBS, 2 * tq, 1), F32),
                        pltpu.VMEM((SB_SUBS, 2 * tq, 1), F32),
                        pltpu.VMEM((SB_SUBS, 2 * tq, LANES), F32)],
        compiler_params=_params(("parallel", "parallel", "arbitrary")),
        name="sb_attention",
    )(qkv, qkv, qkv, tri)


def _mla_attn_body(shift_ref, q_ref, k_ref, v_ref, o_ref, qs_ref, m_ref, acc_ref, *, tq):
    qi = pl.program_id(2)
    shift_max = shift_ref[0]
    lane_q = lax.broadcasted_iota(I32, (1, PAIR_W), 1)
    lane_o = lax.broadcasted_iota(I32, (1, LANES), 1)
    q2 = q_ref[0]
    for hh in range(2):
        nope = (lane_q // NOPE_DIM) == hh
        pe = jnp.logical_and(lane_q >= 2 * NOPE_DIM, ((lane_q - 2 * NOPE_DIM) // ROPE_DIM) == hh)
        keep = jnp.logical_or(jnp.logical_or(nope, pe), lane_q == ONES_LANE + hh)
        qs_ref[hh] = jnp.where(keep, q2, jnp.zeros_like(q2))
        acc_ref[hh] = jnp.zeros((tq, 2 * LANES), F32)

    ones = jnp.ones((tq, LANES), BF16)
    causal = (lax.broadcasted_iota(I32, (tq, tq), 1) <= lax.broadcasted_iota(I32, (tq, tq), 0))

    def scores(hh, j, masked):
        sc = _dot_nt(qs_ref[hh], k_ref[0, pl.ds(pl.multiple_of(j * tq, tq), tq), :])
        return jnp.where(causal, sc, -jnp.inf) if masked else sc

    def values(j):
        return jnp.concatenate([v_ref[0, pl.ds(pl.multiple_of(j * tq, tq), tq), :], ones], axis=-1)

    def static_blocks(blocks):
        for hh in range(2):
            acc_ref[hh] += sum(_dot(jnp.exp2(scores(hh, j, masked)).astype(BF16), values(j))
                               for j, masked in blocks)

    def online_block(j, masked):
        for hh in range(2):
            sc = scores(hh, j, masked)
            m_prev = m_ref[hh]
            m_new = jnp.maximum(m_prev, jnp.max(sc, axis=-1, keepdims=True))
            p = jnp.exp2(sc - m_new)
            acc_ref[hh] = jnp.exp2(m_prev - m_new) * acc_ref[hh] + _dot(p.astype(BF16), values(j))
            m_ref[hh] = m_new

    @pl.when(shift_max <= MLA_STATIC_SHIFT_MAX)
    def _():
        def group(j, c):
            static_blocks([(MLA_GROUP * j + u, False) for u in range(MLA_GROUP)])
            return c
        lax.fori_loop(0, qi // MLA_GROUP, group, 0)

        for rest in range(MLA_GROUP):
            @pl.when(qi % MLA_GROUP == rest)
            def _():
                static_blocks([(qi - rest + u, False) for u in range(rest)] + [(qi, True)])

    @pl.when(shift_max > MLA_STATIC_SHIFT_MAX)
    def _():
        m_ref[...] = jnp.full_like(m_ref, -jnp.inf)

        def body(j, c):
            online_block(j, False)
            return c
        lax.fori_loop(0, qi, body, 0)
        online_block(qi, True)

    outs = [acc_ref[hh][:, :LANES] / acc_ref[hh][:, LANES:] for hh in range(2)]
    o_ref[0] = jnp.where(lane_o < V_DIM, outs[0], outs[1]).astype(o_ref.dtype)


def mla_attention(shift, qcat, kcat, v, tq):
    b, s, _ = qcat.shape
    n_pairs = N_HEADS // 2
    grid_spec = pltpu.PrefetchScalarGridSpec(
        num_scalar_prefetch=1,
        grid=(b, n_pairs, s // tq),
        in_specs=[pl.BlockSpec((1, tq, PAIR_W), lambda bi, p, i, sh: (bi, i, p)),
                  pl.BlockSpec((1, s, PAIR_W), lambda bi, p, i, sh: (bi, 0, p)),
                  pl.BlockSpec((1, s, LANES), lambda bi, p, i, sh: (bi, 0, p))],
        out_specs=pl.BlockSpec((1, tq, LANES), lambda bi, p, i, sh: (bi, i, p)),
        scratch_shapes=[pltpu.VMEM((2, tq, PAIR_W), BF16),
                        pltpu.VMEM((2, tq, 1), F32),
                        pltpu.VMEM((2, tq, 2 * LANES), F32)],
    )
    return pl.pallas_call(
        functools.partial(_mla_attn_body, tq=tq),
        grid_spec=grid_spec,
        out_shape=jax.ShapeDtypeStruct((b, s, N_HEADS * V_DIM), BF16),
        compiler_params=_params(("parallel", "parallel", "arbitrary")),
        name="mla_attention",
    )(shift, qcat, kcat, v)


def _group_rms(x, gmat, group, g):
    chunks = []
    for c in range(x.shape[1] // 256):
        xc = x[:, c * 256:(c + 1) * 256]
        ss = _dot_hilo(xc * xc, gmat)
        chunks.append(xc * lax.rsqrt(ss * (1.0 / group) + EPS))
    y = chunks[0] if len(chunks) == 1 else jnp.concatenate(chunks, axis=-1)
    return y * g


def _rope128(x, cos, sin):
    lane = lax.broadcasted_iota(I32, (1, LANES), 1)
    first_half = (lane % ROPE_DIM) < (ROPE_DIM // 2)
    rot = jnp.where(first_half,
                    -pltpu.roll(x, LANES - ROPE_DIM // 2, 1),
                    pltpu.roll(x, ROPE_DIM // 2, 1))
    return x * cos + rot * sin


def _kv_prep_body(kva_ref, g_a_ref, g_pe_ref, cos_ref, sin_ref, wn_ref, wv_ref, g_n_ref,
                  g64_ref, k_ref, v_ref):
    lane = lax.broadcasted_iota(I32, (1, LANES), 1)
    kva = kva_ref[...]
    c = kva[:, :KV_RANK]
    cn = (c * lax.rsqrt(jnp.mean(c * c, axis=-1, keepdims=True) + EPS) * g_a_ref[...]).astype(BF16)
    pe = kva[:, KV_RANK:KV_RANK + LANES]
    ms = jnp.sum(pe * pe, axis=-1, keepdims=True) * (1.0 / ROPE_DIM)
    pen = pe * lax.rsqrt(ms + EPS) * g_pe_ref[...]
    kpe = _rope128(pen, cos_ref[...], sin_ref[...])
    kpe = jnp.where(lane < ROPE_DIM, kpe, 0.0)
    kpe2 = kpe + pltpu.roll(kpe, ROPE_DIM, 1)
    ones_lanes = jnp.logical_or(lane == ONES_LANE - LANES, lane == ONES_LANE + 1 - LANES)
    kpe2 = jnp.where(ones_lanes, 1.0, kpe2).astype(BF16)
    kn = _group_rms(_dot(cn, wn_ref[...]), g64_ref[...], NOPE_DIM, g_n_ref[...]).astype(BF16)
    v_ref[...] = _dot(cn, wv_ref[...]).astype(BF16)
    for p in range(N_HEADS // 2):
        k_ref[:, p * PAIR_W:p * PAIR_W + LANES] = kn[:, p * LANES:(p + 1) * LANES]
        k_ref[:, p * PAIR_W + LANES:(p + 1) * PAIR_W] = kpe2


def _q_prep_body(qa_ref, g_a_ref, wn_ref, wp_ref, g_n_ref, g_p_ref, cos_ref, sin_ref,
                 g64_ref, g32_ref, shift_ref, q_ref, *, scale):
    lane = lax.broadcasted_iota(I32, (1, LANES), 1)
    shift_lanes = jnp.logical_or(lane == ONES_LANE - LANES, lane == ONES_LANE + 1 - LANES)
    neg_shift = -shift_ref[...]
    qa = qa_ref[...]
    qan = (qa * lax.rsqrt(jnp.mean(qa * qa, axis=-1, keepdims=True) + EPS) * g_a_ref[...]).astype(BF16)
    qn = _group_rms(_dot(qan, wn_ref[...]), g64_ref[...], NOPE_DIM, g_n_ref[...]) * scale
    qp = _group_rms(_dot(qan, wp_ref[...]), g32_ref[...], ROPE_DIM, g_p_ref[...])
    cos = cos_ref[...]
    sin = sin_ref[...]
    for c in range(N_HEADS * ROPE_DIM // LANES):
        pe4 = _rope128(qp[:, c * LANES:(c + 1) * LANES], cos, sin) * scale
        for half in range(2):
            p = 2 * c + half
            x = pe4 if half == 0 else pltpu.roll(pe4, 2 * ROPE_DIM, 1)
            q_ref[:, p * PAIR_W:p * PAIR_W + LANES] = qn[:, p * LANES:(p + 1) * LANES].astype(BF16)
            x = jnp.where(lane < 2 * ROPE_DIM, x, jnp.where(shift_lanes, neg_shift, 0.0))
            q_ref[:, p * PAIR_W + LANES:(p + 1) * PAIR_W] = x.astype(BF16)


def _block_diag_ones(n, group):
    a = lax.broadcasted_iota(I32, (n, n), 0) // group
    b = lax.broadcasted_iota(I32, (n, n), 1) // group
    return (a == b).astype(BF16)


def _full(shape):
    return pl.BlockSpec(shape, lambda i: (0,) * len(shape))


def kv_prep(kva, g_a, g_pe_pad, cos_t, sin_t, wn, wv, g_n_t, tm, seq):
    t = kva.shape[0]
    nseq = seq // tm
    return pl.pallas_call(
        _kv_prep_body,
        grid=(t // tm,),
        in_specs=[pl.BlockSpec((tm, kva.shape[1]), lambda i: (i, 0)),
                  _full((1, KV_RANK)), _full((1, LANES)),
                  pl.BlockSpec((tm, LANES), lambda i: (i % nseq, 0)),
                  pl.BlockSpec((tm, LANES), lambda i: (i % nseq, 0)),
                  _full(wn.shape), _full(wv.shape), _full((1, wn.shape[1])),
                  _full((256, 256))],
        out_specs=[pl.BlockSpec((tm, (N_HEADS // 2) * PAIR_W), lambda i: (i, 0)),
                   pl.BlockSpec((tm, N_HEADS * V_DIM), lambda i: (i, 0))],
        out_shape=[jax.ShapeDtypeStruct((t, (N_HEADS // 2) * PAIR_W), BF16),
                   jax.ShapeDtypeStruct((t, N_HEADS * V_DIM), BF16)],
        compiler_params=_params(("parallel",)),
        name="kv_prep",
    )(kva, g_a, g_pe_pad, cos_t, sin_t, wn, wv, g_n_t, _block_diag_ones(256, NOPE_DIM))


def q_prep(qa, g_a, wn, wp, g_n_t, g_p_t, cos_t, sin_t, shift, tm, seq, scale):
    t = qa.shape[0]
    nseq = seq // tm
    return pl.pallas_call(
        functools.partial(_q_prep_body, scale=scale),
        grid=(t // tm,),
        in_specs=[pl.BlockSpec((tm, qa.shape[1]), lambda i: (i, 0)),
                  _full((1, qa.shape[1])), _full(wn.shape), _full(wp.shape),
                  _full((1, wn.shape[1])), _full((1, wp.shape[1])),
                  pl.BlockSpec((tm, LANES), lambda i: (i % nseq, 0)),
                  pl.BlockSpec((tm, LANES), lambda i: (i % nseq, 0)),
                  _full((256, 256)), _full((256, 256)), _full((1, 1))],
        out_specs=pl.BlockSpec((tm, (N_HEADS // 2) * PAIR_W), lambda i: (i, 0)),
        out_shape=jax.ShapeDtypeStruct((t, (N_HEADS // 2) * PAIR_W), BF16),
        compiler_params=_params(("parallel",)),
        name="q_prep",
    )(qa, g_a, wn, wp, g_n_t, g_p_t, cos_t, sin_t,
      _block_diag_ones(256, NOPE_DIM), _block_diag_ones(256, ROPE_DIM), shift.reshape(1, 1))


def _ffn_body(att_ref, wo_ref, res_ref, g_ref, wg_ref, wu_ref, wd_ref, o_ref, h_ref):
    x = res_ref[...] + _dot(att_ref[...], wo_ref[...])
    inv = lax.rsqrt(jnp.mean(x * x, axis=-1, keepdims=True) + EPS)
    xn = (x * inv * g_ref[...]).astype(BF16)
    _swiglu_hidden(xn, wg_ref, wu_ref, h_ref, ())
    o_ref[...] = x + _dot(h_ref[...], wd_ref[...])


def dense_ffn(att, wo, res, g, wg, wu, wd, layer, tm):
    t, d = res.shape
    f = wg.shape[2]
    once = pl.Buffered(1)
    return pl.pallas_call(
        _ffn_body,
        grid=(t // tm,),
        in_specs=[pl.BlockSpec((tm, att.shape[1]), lambda i: (i, 0)),
                  pl.BlockSpec(wo.shape, lambda i: (0, 0), pipeline_mode=once),
                  pl.BlockSpec((tm, d), lambda i: (i, 0)),
                  pl.BlockSpec((1, d), lambda i: (0, 0)),
                  pl.BlockSpec((None, d, f), lambda i: (layer, 0, 0), pipeline_mode=once),
                  pl.BlockSpec((None, d, f), lambda i: (layer, 0, 0), pipeline_mode=once),
                  pl.BlockSpec((None, f, d), lambda i: (layer, 0, 0), pipeline_mode=once)],
        out_specs=pl.BlockSpec((tm, d), lambda i: (i, 0)),
        out_shape=jax.ShapeDtypeStruct((t, d), F32),
        scratch_shapes=[pltpu.VMEM((tm, f), BF16)],
        compiler_params=_params(("parallel",)),
        name="dense_ffn",
    )(att, wo, res, g.reshape(1, d), wg, wu, wd)


def _router_body(att_ref, wo_ref, res_ref, g_ref, wr_ref, b_ref, upper_ref,
                 x_ref, xn_ref, idx_ref, gate_ref, rank_ref, cnt_ref, base_ref):
    i = pl.program_id(0)

    @pl.when(i == 0)
    def _():
        base_ref[...] = jnp.zeros_like(base_ref)

    x = res_ref[...] + _dot(att_ref[...], wo_ref[...])
    x_ref[...] = x
    xn = x * lax.rsqrt(jnp.mean(x * x, axis=-1, keepdims=True) + EPS) * g_ref[...]
    xn_ref[...] = xn
    tm = x.shape[0]

    x_hi, x_lo = _split_bf16(xn)
    w_hi, w_lo = _split_bf16(wr_ref[...])
    logits = _dot_nt(w_hi, x_hi) + _dot_nt(w_hi, x_lo) + _dot_nt(w_lo, x_hi) + b_ref[...]

    e_iota = lax.broadcasted_iota(I32, (N_EXPERTS, tm), 0)
    m1 = jnp.max(logits, axis=0, keepdims=True)
    i1 = jnp.min(jnp.where(logits == m1, e_iota, N_EXPERTS), axis=0, keepdims=True)
    sel1 = e_iota == i1
    rest = jnp.where(sel1, -jnp.inf, logits)
    m2 = jnp.max(rest, axis=0, keepdims=True)
    i2 = jnp.min(jnp.where(rest == m2, e_iota, N_EXPERTS), axis=0, keepdims=True)
    sel2 = e_iota == i2
    e2 = jnp.exp(m2 - m1)
    g1 = 1.0 / (1.0 + e2)
    idx_ref[...] = jnp.concatenate([i1, i2], axis=0)
    gate_ref[...] = jnp.concatenate([g1, e2 * g1], axis=0)

    member = jnp.logical_or(sel1, sel2)
    prefix = _dot(member.astype(BF16), upper_ref[...])
    rank = prefix + base_ref[...]
    r1 = jnp.sum(jnp.where(sel1, rank, 0.0), axis=0, keepdims=True)
    r2 = jnp.sum(jnp.where(sel2, rank, 0.0), axis=0, keepdims=True)
    rank_ref[...] = jnp.concatenate([r1, r2], axis=0).astype(I32)
    base_ref[...] += jnp.sum(member.astype(F32), axis=1, keepdims=True)
    cnt_ref[...] = jnp.broadcast_to(base_ref[...], cnt_ref.shape).astype(I32)


def moe_router(att, wo, res, g, w_router, b_router, tm):
    t, d = res.shape
    a = lax.broadcasted_iota(I32, (tm, tm), 0)
    b = lax.broadcasted_iota(I32, (tm, tm), 1)
    upper = (a < b).astype(BF16)
    return pl.pallas_call(
        _router_body,
        grid=(t // tm,),
        in_specs=[pl.BlockSpec((tm, att.shape[1]), lambda i: (i, 0)), _full(wo.shape),
                  pl.BlockSpec((tm, d), lambda i: (i, 0)),
                  _full((1, d)), _full((N_EXPERTS, d)), _full((N_EXPERTS, 1)), _full((tm, tm))],
        out_specs=[pl.BlockSpec((tm, d), lambda i: (i, 0)),
                   pl.BlockSpec((tm, d), lambda i: (i, 0)),
                   pl.BlockSpec((2, tm), lambda i: (0, i)),
                   pl.BlockSpec((2, tm), lambda i: (0, i)),
                   pl.BlockSpec((2, tm), lambda i: (0, i)),
                   _full((N_EXPERTS, LANES))],
        out_shape=[jax.ShapeDtypeStruct((t, d), F32),
                   jax.ShapeDtypeStruct((t, d), F32),
                   jax.ShapeDtypeStruct((2, t), I32),
                   jax.ShapeDtypeStruct((2, t), F32),
                   jax.ShapeDtypeStruct((2, t), I32),
                   jax.ShapeDtypeStruct((N_EXPERTS, LANES), I32)],
        scratch_shapes=[pltpu.VMEM((N_EXPERTS, 1), F32)],
        compiler_params=_params(("arbitrary",)),
        name="moe_router",
    )(att, wo, res, g.reshape(1, d), w_router.T, b_router.reshape(N_EXPERTS, 1), upper)


def _row_copy(src, src_row, dst, dst_row, sem):
    return pltpu.make_async_copy(src.at[pl.ds(src_row, 1)], dst.at[pl.ds(dst_row, 1)], sem)


def _dispatch_body(dest_ref, xn_ref, buf_in_hbm, buf_hbm, sem, *, tb, t_total):
    del buf_in_hbm
    base = pl.program_id(0) * tb

    def issue(t, c):
        for k in range(2):
            _row_copy(xn_ref, t, buf_hbm, dest_ref[k * t_total + base + t], sem).start()
        return c

    lax.fori_loop(0, tb, issue, 0, unroll=8)

    def drain(t, c):
        for k in range(2):
            _row_copy(xn_ref, 0, buf_hbm, 0, sem).wait()
        return c

    lax.fori_loop(0, tb, drain, 0, unroll=8)


def moe_dispatch(dest_flat, xn, n_rows, tb):
    t, d = xn.shape
    buf0 = jnp.zeros((n_rows, d), xn.dtype)
    grid_spec = pltpu.PrefetchScalarGridSpec(
        num_scalar_prefetch=1,
        grid=(t // tb,),
        in_specs=[pl.BlockSpec((tb, d), lambda i, dest: (i, 0)), pl.BlockSpec(memory_space=pl.ANY)],
        out_specs=pl.BlockSpec(memory_space=pl.ANY),
        scratch_shapes=[pltpu.SemaphoreType.DMA(())],
    )
    return pl.pallas_call(
        functools.partial(_dispatch_body, tb=tb, t_total=t),
        grid_spec=grid_spec,
        out_shape=jax.ShapeDtypeStruct((n_rows, d), xn.dtype),
        input_output_aliases={2: 0},
        compiler_params=_params(("arbitrary",)),
        name="moe_dispatch",
    )(dest_flat, xn, buf0)


def _swiglu_hidden(xb, wg_ref, wu_ref, h_ref, widx):
    f = h_ref.shape[1]
    for c0 in range(0, f, MXU_TILE):
        c1 = min(c0 + MXU_TILE, f)
        gate = _dot(xb, wg_ref[widx + (slice(None), slice(c0, c1))])
        up = _dot(xb, wu_ref[widx + (slice(None), slice(c0, c1))])
        h_ref[:, c0:c1] = (gate * jax.nn.sigmoid(gate) * up).astype(BF16)


def _experts_body(be_ref, nv_ref, x_ref, wg_ref, wu_ref, wd_ref, o_ref, h_ref):
    del be_ref
    valid = pl.program_id(0) < nv_ref[0]

    @pl.when(valid)
    def _():
        _swiglu_hidden(x_ref[...].astype(BF16), wg_ref, wu_ref, h_ref, (0,))
        o_ref[...] = _dot(h_ref[...], wd_ref[0])

    @pl.when(jnp.logical_not(valid))
    def _():
        o_ref[...] = jnp.zeros_like(o_ref)


def moe_experts(block_e, n_valid, buf, wg, wu, wd, layer, tm):
    n_rows, d = buf.shape
    f = wg.shape[3]
    once = pl.Buffered(1)
    grid_spec = pltpu.PrefetchScalarGridSpec(
        num_scalar_prefetch=2,
        grid=(n_rows // tm,),
        in_specs=[pl.BlockSpec((tm, d), lambda i, be, nv: (i, 0)),
                  pl.BlockSpec((None, 1, d, f), lambda i, be, nv: (layer, be[i], 0, 0), pipeline_mode=once),
                  pl.BlockSpec((None, 1, d, f), lambda i, be, nv: (layer, be[i], 0, 0), pipeline_mode=once),
                  pl.BlockSpec((None, 1, f, d), lambda i, be, nv: (layer, be[i], 0, 0), pipeline_mode=once)],
        out_specs=pl.BlockSpec((tm, d), lambda i, be, nv: (i, 0)),
        scratch_shapes=[pltpu.VMEM((tm, f), BF16)],
    )
    return pl.pallas_call(
        _experts_body,
        grid_spec=grid_spec,
        out_shape=jax.ShapeDtypeStruct((n_rows, d), F32),
        compiler_params=_params(("arbitrary",), vmem=VMEM_LIMIT_BIG),
        name="moe_experts",
    )(block_e, n_valid, buf, wg, wu, wd)


def _combine_body(dest_ref, x_ref, gate_ref, eo_hbm, o_ref, buf_ref, sems, *, tb, t_total):
    i = pl.program_id(0)
    slot = i % 2

    def start_block(block, to_slot):
        base = block * tb

        def issue(t, c):
            for k in range(2):
                _row_copy(eo_hbm, dest_ref[k * t_total + base + t], buf_ref.at[to_slot, k], t,
                          sems.at[to_slot]).start()
            return c

        lax.fori_loop(0, tb, issue, 0, unroll=8)

    @pl.when(i == 0)
    def _():
        start_block(0, 0)

    @pl.when(i + 1 < pl.num_programs(0))
    def _():
        start_block(i + 1, 1 - slot)

    def drain(t, c):
        for k in range(2):
            _row_copy(eo_hbm, 0, buf_ref.at[slot, k], 0, sems.at[slot]).wait()
        return c

    lax.fori_loop(0, tb, drain, 0, unroll=8)
    gates = gate_ref[...]
    o_ref[...] = x_ref[...] + gates[:, 0:1] * buf_ref[slot, 0] + gates[:, 1:2] * buf_ref[slot, 1]


def moe_combine(dest_flat, x, gates_t, expert_out, tb):
    t, d = x.shape
    grid_spec = pltpu.PrefetchScalarGridSpec(
        num_scalar_prefetch=1,
        grid=(t // tb,),
        in_specs=[pl.BlockSpec((tb, d), lambda i, dest: (i, 0)),
                  pl.BlockSpec((tb, 2), lambda i, dest: (i, 0)),
                  pl.BlockSpec(memory_space=pl.ANY)],
        out_specs=pl.BlockSpec((tb, d), lambda i, dest: (i, 0)),
        scratch_shapes=[pltpu.VMEM((2, 2, tb, d), F32), pltpu.SemaphoreType.DMA((2,))],
    )
    return pl.pallas_call(
        functools.partial(_combine_body, tb=tb, t_total=t),
        grid_spec=grid_spec,
        out_shape=jax.ShapeDtypeStruct((t, d), F32),
        compiler_params=_params(("arbitrary",)),
        name="moe_combine",
    )(dest_flat, x, gates_t, expert_out)


def moe_layer(att, wo, res, g, w_router, b_router, wg, wu, wd, layer, tiles):
    t, d = res.shape
    tm = tiles["moe_tm"]
    x, xn, idx, gates, rank, cnt = moe_router(att, wo, res, g, w_router, b_router, tiles["router_tm"])
    counts = cnt[:, 0]
    padded = ((counts + tm - 1) // tm) * tm
    pends = jnp.cumsum(padded)
    pstarts = pends - padded
    group_start = sum(jnp.where(idx == e, pstarts[e], 0) for e in range(N_EXPERTS))
    dest_flat = (group_start + rank).reshape(-1)
    n_rows = 2 * t + N_EXPERTS * tm
    n_blocks = n_rows // tm
    block_row = jnp.arange(n_blocks, dtype=I32) * tm
    block_e = jnp.minimum(sum((block_row >= pends[e]).astype(I32) for e in range(N_EXPERTS)),
                          N_EXPERTS - 1)
    n_valid = (pends[-1:] // tm).astype(I32)
    buf = moe_dispatch(dest_flat, xn, n_rows, tiles["moe_tb"])
    eo = moe_experts(block_e, n_valid, buf, wg, wu, wd, layer, tm)
    return moe_combine(dest_flat, x, gates.T, eo, tiles["moe_tb"])


def _tiles(t, s):
    def fit(n, want):
        while n % want:
            want //= 2
        return want
    return dict(
        proj_tm=fit(t, 1024), attn_tq=fit(s, 256), mla_tq=fit(s, 512), prep_tm=fit(s, 512),
        ffn_tm=fit(t, 512), router_tm=fit(t, 512), moe_tm=fit(t, 512), moe_tb=fit(t, 512),
    )


def _mla_score_bound(g_q_nope, g_q_pe, g_k_nope, g_k_pe, scale):
    q2 = NOPE_DIM * jnp.max(g_q_nope * g_q_nope) + ROPE_DIM * jnp.max(g_q_pe * g_q_pe)
    k2 = NOPE_DIM * jnp.max(g_k_nope * g_k_nope) + ROPE_DIM * jnp.max(g_k_pe * g_k_pe)
    return (scale * jnp.sqrt(q2 * k2) * 1.02 + 1e-2).reshape(1)


def _rope_tables(seq):
    pos = jnp.arange(seq, dtype=F32)
    inv_freq = ROPE_THETA ** (-jnp.arange(0, ROPE_DIM, 2, dtype=F32) / ROPE_DIM)
    ang = pos[:, None] * inv_freq[None, :]
    ang = jnp.concatenate([ang, ang], axis=-1)
    reps = LANES // ROPE_DIM
    return jnp.tile(jnp.cos(ang), (1, reps)), jnp.tile(jnp.sin(ang), (1, reps))


def kernel(x, g_mix, g_ffn, sb_w_qkv, sb_w_o, kv_g_src, kv_w_a, kv_g_a, kv_w_b, kv_g_k_nope, kv_g_k_pe, mla_w_q_a, mla_g_q_a, mla_w_q_b, mla_g_q_nope, mla_g_q_pe, mla_w_o, ffn_w_gate, ffn_w_up, ffn_w_down, moe_w_router, moe_b_router, moe_w_gate, moe_w_up, moe_w_down):
    b, s, d = x.shape
    t = b * s
    depth = g_mix.shape[0]
    n_a = sb_w_qkv.shape[0]
    tiles = _tiles(t, s)
    cos_t, sin_t = _rope_tables(s)
    h = x.reshape(t, d)
    hw = N_HEADS * SB_HEAD_DIM
    kcat = vcat = None
    ffn_w = [w.astype(BF16) for w in (ffn_w_gate, ffn_w_up, ffn_w_down)]
    moe_w = [w.astype(BF16) for w in (moe_w_gate, moe_w_up, moe_w_down)]

    for i in range(depth):
        if i < n_a:
            col_scale = jnp.concatenate([jnp.full((hw,), LOG2E / math.sqrt(SB_HEAD_DIM), F32),
                                         jnp.ones((2 * hw,), F32)])
            w_qkv = (sb_w_qkv[i] * col_scale).astype(BF16)
            qkv = rms_matmul(h, g_mix[i], w_qkv, BF16, tiles["ffn_tm"], w_qkv.shape[1])
            att = sb_attention(qkv.reshape(b, s, 3 * hw), tiles["attn_tq"])
            w_o = sb_w_o[i].astype(BF16)
        else:
            j = i - n_a
            if kcat is None:
                pad = jnp.zeros((d, LANES - ROPE_DIM), F32)
                w_a = jnp.concatenate([kv_w_a, pad], axis=1).astype(BF16)
                kva = rms_matmul(h, kv_g_src, w_a, F32, tiles["proj_tm"], w_a.shape[1])
                w_b = kv_w_b.reshape(KV_RANK, N_HEADS, NOPE_DIM + V_DIM)
                wn = w_b[:, :, :NOPE_DIM].reshape(KV_RANK, N_HEADS * NOPE_DIM).astype(BF16)
                wv = w_b[:, :, NOPE_DIM:].reshape(KV_RANK, N_HEADS * V_DIM).astype(BF16)
                g_pe_pad = jnp.concatenate([kv_g_k_pe, jnp.zeros((LANES - ROPE_DIM,), F32)]).reshape(1, LANES)
                kcat, vcat = kv_prep(kva, kv_g_a.reshape(1, KV_RANK), g_pe_pad, cos_t, sin_t, wn, wv,
                                     jnp.tile(kv_g_k_nope, N_HEADS).reshape(1, -1), tiles["prep_tm"], s)
                kcat = kcat.reshape(b, s, -1)
                vcat = vcat.reshape(b, s, -1)
            qa = rms_matmul(h, g_mix[i], mla_w_q_a[j].astype(BF16), F32, tiles["proj_tm"], mla_w_q_a.shape[2])
            w_qb = mla_w_q_b[j].reshape(-1, N_HEADS, NOPE_DIM + ROPE_DIM)
            wqn = w_qb[:, :, :NOPE_DIM].reshape(-1, N_HEADS * NOPE_DIM).astype(BF16)
            wqp = w_qb[:, :, NOPE_DIM:].reshape(-1, N_HEADS * ROPE_DIM).astype(BF16)
            scale = LOG2E / math.sqrt(NOPE_DIM + ROPE_DIM)
            shift = _mla_score_bound(mla_g_q_nope[j], mla_g_q_pe[j], kv_g_k_nope, kv_g_k_pe, scale)
            qcat = q_prep(qa, mla_g_q_a[j].reshape(1, -1), wqn, wqp,
                          jnp.tile(mla_g_q_nope[j], N_HEADS).reshape(1, -1),
                          jnp.tile(mla_g_q_pe[j], N_HEADS).reshape(1, -1),
                          cos_t, sin_t, shift, tiles["prep_tm"], s, scale)
            att = mla_attention(shift, qcat.reshape(b, s, -1), kcat, vcat, tiles["mla_tq"])
            w_o = mla_w_o[j].astype(BF16)

        att = att.reshape(t, -1)
        m = i // 2
        if i % 2 == 0:
            h = dense_ffn(att, w_o, h, g_ffn[i], *ffn_w, m, tiles["ffn_tm"])
        else:
            h = moe_layer(att, w_o, h, g_ffn[i], moe_w_router[m], moe_b_router[m], *moe_w, m, tiles)
    return h.reshape(b, s, d)
```

```python
import functools
import math

import jax
import jax.numpy as jnp
from jax import lax
from jax.experimental import pallas as pl
from jax.experimental.pallas import tpu as pltpu

F32 = jnp.float32
BF16 = jnp.bfloat16
I32 = jnp.int32

EPS = 1e-6
N_HEADS = 16
SB_HEAD_DIM = 64
NOPE_DIM = 64
ROPE_DIM = 32
V_DIM = 64
KV_RANK = 256
ROPE_THETA = 10000.0
N_EXPERTS = 8
LANES = 128
PAIR_W = 2 * NOPE_DIM + LANES
ONES_LANE = 2 * NOPE_DIM + 2 * ROPE_DIM
MLA_STATIC_SHIFT_MAX = 50.0
MLA_GROUP = 4
MLA_SUBS = 2
LOG2E = 1.4426950408889634
SB_SKIP_EXP = 110.0
SB_SUBS = 8
SB_MASKED_LOGIT = -1e30
MXU_TILE = 256
VMEM_LIMIT = 52 * 1024 * 1024
VMEM_LIMIT_BIG = 58 * 1024 * 1024


def _params(sem, vmem=VMEM_LIMIT):
    return pltpu.CompilerParams(dimension_semantics=sem, vmem_limit_bytes=vmem)


def _dot(a, b):
    return jnp.dot(a, b, preferred_element_type=F32)


def _dot_nt(a, b):
    return lax.dot_general(a, b, (((1,), (1,)), ((), ())), preferred_element_type=F32)


def _split_bf16(x):
    hi = x.astype(BF16)
    lo = (x - hi.astype(F32)).astype(BF16)
    return hi, lo


def _dot_hilo(x, w):
    hi, lo = _split_bf16(x)
    return _dot(hi, w) + _dot(lo, w)


def _rms_matmul_body(x_ref, g_ref, w_ref, o_ref, xn_ref):
    @pl.when(pl.program_id(1) == 0)
    def _():
        x = x_ref[...]
        inv = lax.rsqrt(jnp.mean(x * x, axis=-1, keepdims=True) + EPS)
        xn_ref[...] = (x * inv * g_ref[...]).astype(BF16)

    o_ref[...] = _dot(xn_ref[...], w_ref[...]).astype(o_ref.dtype)


def rms_matmul(x, g, w, out_dtype, tm, tn):
    t, d = x.shape
    n = w.shape[1]
    w_mode = pl.Buffered(1) if tn == n else None
    return pl.pallas_call(
        _rms_matmul_body,
        grid=(t // tm, n // tn),
        in_specs=[pl.BlockSpec((tm, d), lambda i, j: (i, 0)),
                  pl.BlockSpec((1, d), lambda i, j: (0, 0)),
                  pl.BlockSpec((d, tn), lambda i, j: (0, j), pipeline_mode=w_mode)],
        out_specs=pl.BlockSpec((tm, tn), lambda i, j: (i, j)),
        out_shape=jax.ShapeDtypeStruct((t, n), out_dtype),
        scratch_shapes=[pltpu.VMEM((tm, d), BF16)],
        compiler_params=_params(("parallel", "arbitrary")),
        name="rms_matmul",
    )(x, g.reshape(1, d), w)


def _softplus2(z):
    return jnp.maximum(jnp.log2(1.0 + jnp.exp2(jnp.minimum(z, 64.0))), z)


def _key_norm_max(k_ref, head_masks, kmax_ref):
    k = k_ref[0].astype(F32)
    k2 = k * k
    for hh, mask in enumerate(head_masks):
        n2 = jnp.sum(jnp.where(mask, k2, 0.0), axis=-1, keepdims=True)
        kmax_ref[hh] = jnp.max(n2, axis=0, keepdims=True)


def _logit_bound(qm, kmax):
    qf = qm.astype(F32)
    return jnp.sqrt(jnp.sum(qf * qf, axis=-1, keepdims=True) * kmax) * 1.01 + 1e-2


def _sb_attn_body(q_ref, k_ref, v_ref, tri_ref, o_ref, kmax_ref, qs_ref, zb_ref, carry_ref, acc_ref,
                  *, tq):
    step = pl.program_id(2)
    lane = lax.broadcasted_iota(I32, (1, LANES), 1)
    head_masks = [(lane // SB_HEAD_DIM) == hh for hh in range(2)]

    @pl.when(step == 0)
    def _():
        _key_norm_max(k_ref, head_masks, kmax_ref)

    for sub in range(SB_SUBS):
        q2 = q_ref[0, sub * tq:(sub + 1) * tq, :]
        for hh in range(2):
            qm = jnp.where(head_masks[hh], q2, jnp.zeros_like(q2))
            qs_ref[sub, hh * tq:(hh + 1) * tq, :] = qm
            zb_ref[sub, hh * tq:(hh + 1) * tq, :] = _logit_bound(qm, kmax_ref[hh])

    row = lax.broadcasted_iota(I32, (2 * tq, tq), 0) % tq
    col = lax.broadcasted_iota(I32, (2 * tq, tq), 1)
    causal = col < row

    def chain(c, start, masked):
        z = _dot_nt(qs_ref[c], k_ref[0, pl.ds(start, tq), :])
        if masked:
            z = jnp.where(causal, z, SB_MASKED_LOGIT)
        sp = _softplus2(z)
        csum = _dot(sp.astype(BF16), tri_ref[...])
        e = jnp.exp2(z - csum)
        return _dot(e.astype(BF16), v_ref[0, pl.ds(start, tq), :]), csum[:, 0:1]

    for sub in range(SB_SUBS):
        qi = SB_SUBS * step + sub
        has_prev = qi >= 1
        prev_start = pl.multiple_of(jnp.maximum(qi - 1, 0) * tq, tq)
        pv_d, tot_d = chain(sub, pl.multiple_of(qi * tq, tq), True)
        pv_p, tot_p = chain(sub, prev_start, False)
        acc_ref[sub] = pv_d + jnp.where(has_prev, jnp.exp2(-tot_d), 0.0) * pv_p
        carry_ref[sub] = tot_d + jnp.where(has_prev, tot_p, 0.0)

    for sub in range(SB_SUBS):
        qi = SB_SUBS * step + sub

        def cond(jj):
            live = jnp.max(zb_ref[sub] - carry_ref[sub]) > -SB_SKIP_EXP * LOG2E
            return jnp.logical_and(jj <= qi, live)

        def body(jj):
            pv, tot = chain(sub, pl.multiple_of((qi - jj) * tq, tq), False)
            carry = carry_ref[sub]
            acc_ref[sub] += jnp.exp2(-carry) * pv
            carry_ref[sub] = carry + tot
            return jj + 1

        lax.while_loop(cond, body, jnp.int32(2))
        acc = acc_ref[sub]
        o_ref[0, sub * tq:(sub + 1) * tq, :] = jnp.where(
            head_masks[0], acc[:tq], acc[tq:]).astype(o_ref.dtype)


def sb_attention(qkv, tq):
    b, s, _ = qkv.shape
    n_pairs = N_HEADS // 2
    rows = SB_SUBS * tq
    ji = lax.broadcasted_iota(I32, (tq, tq), 0)
    si = lax.broadcasted_iota(I32, (tq, tq), 1)
    tri = (ji >= si).astype(BF16)
    return pl.pallas_call(
        functools.partial(_sb_attn_body, tq=tq),
        grid=(b, n_pairs, s // rows),
        in_specs=[pl.BlockSpec((1, rows, LANES), lambda bi, p, i: (bi, i, p)),
                  pl.BlockSpec((1, s, LANES), lambda bi, p, i: (bi, 0, n_pairs + p)),
                  pl.BlockSpec((1, s, LANES), lambda bi, p, i: (bi, 0, 2 * n_pairs + p)),
                  pl.BlockSpec((tq, tq), lambda bi, p, i: (0, 0))],
        out_specs=pl.BlockSpec((1, rows, LANES), lambda bi, p, i: (bi, i, p)),
        out_shape=jax.ShapeDtypeStruct((b, s, N_HEADS * SB_HEAD_DIM), BF16),
        scratch_shapes=[pltpu.VMEM((2, 1, 1), F32),
                        pltpu.VMEM((SB_SUBS, 2 * tq, LANES), BF16),
                        pltpu.VMEM((SB_SUBS, 2 * tq, 1), F32),
                        pltpu.VMEM((SB_SUBS, 2 * tq, 1), F32),
                        pltpu.VMEM((SB_SUBS, 2 * tq, LANES), F32)],
        compiler_params=_params(("parallel", "parallel", "arbitrary")),
        name="sb_attention",
    )(qkv, qkv, qkv, tri)


def _mla_attn_body(shift_ref, q_ref, k_ref, v_ref, o_ref, qs_ref, m_ref, acc_ref, *, tq):
    step = pl.program_id(2)
    for sub in range(MLA_SUBS):
        rows = slice(sub * tq, (sub + 1) * tq)
        o_ref[0, rows, :] = _mla_query_block(MLA_SUBS * step + sub, q_ref[0, rows, :], shift_ref[0],
                                             k_ref, v_ref, qs_ref, m_ref, acc_ref, tq).astype(o_ref.dtype)


def _mla_query_block(qi, q2, shift_max, k_ref, v_ref, qs_ref, m_ref, acc_ref, tq):
    lane_q = lax.broadcasted_iota(I32, (1, PAIR_W), 1)
    lane_o = lax.broadcasted_iota(I32, (1, LANES), 1)
    for hh in range(2):
        nope = (lane_q // NOPE_DIM) == hh
        pe = jnp.logical_and(lane_q >= 2 * NOPE_DIM, ((lane_q - 2 * NOPE_DIM) // ROPE_DIM) == hh)
        keep = jnp.logical_or(jnp.logical_or(nope, pe), lane_q == ONES_LANE + hh)
        qs_ref[hh] = jnp.where(keep, q2, jnp.zeros_like(q2))
        acc_ref[hh] = jnp.zeros((tq, 2 * LANES), F32)

    ones = jnp.ones((tq, LANES), BF16)
    causal = (lax.broadcasted_iota(I32, (tq, tq), 1) <= lax.broadcasted_iota(I32, (tq, tq), 0))

    def scores(hh, j, masked):
        sc = _dot_nt(qs_ref[hh], k_ref[0, pl.ds(pl.multiple_of(j * tq, tq), tq), :])
        return jnp.where(causal, sc, -jnp.inf) if masked else sc

    def values(j):
        return jnp.concatenate([v_ref[0, pl.ds(pl.multiple_of(j * tq, tq), tq), :], ones], axis=-1)

    def static_blocks(blocks):
        for hh in range(2):
            acc_ref[hh] += sum(_dot(jnp.exp2(scores(hh, j, masked)).astype(BF16), values(j))
                               for j, masked in blocks)

    def online_block(j, masked):
        for hh in range(2):
            sc = scores(hh, j, masked)
            m_prev = m_ref[hh]
            m_new = jnp.maximum(m_prev, jnp.max(sc, axis=-1, keepdims=True))
            p = jnp.exp2(sc - m_new)
            acc_ref[hh] = jnp.exp2(m_prev - m_new) * acc_ref[hh] + _dot(p.astype(BF16), values(j))
            m_ref[hh] = m_new

    @pl.when(shift_max <= MLA_STATIC_SHIFT_MAX)
    def _():
        def group(j, c):
            static_blocks([(MLA_GROUP * j + u, False) for u in range(MLA_GROUP)])
            return c
        lax.fori_loop(0, qi // MLA_GROUP, group, 0)

        for rest in range(MLA_GROUP):
            @pl.when(qi % MLA_GROUP == rest)
            def _():
                static_blocks([(qi - rest + u, False) for u in range(rest)] + [(qi, True)])

    @pl.when(shift_max > MLA_STATIC_SHIFT_MAX)
    def _():
        m_ref[...] = jnp.full_like(m_ref, -jnp.inf)

        def body(j, c):
            online_block(j, False)
            return c
        lax.fori_loop(0, qi, body, 0)
        online_block(qi, True)

    outs = [acc_ref[hh][:, :LANES] / acc_ref[hh][:, LANES:] for hh in range(2)]
    return jnp.where(lane_o < V_DIM, outs[0], outs[1])


def mla_attention(shift, qcat, kcat, v, tq):
    b, s, _ = qcat.shape
    n_pairs = N_HEADS // 2
    rows = MLA_SUBS * tq
    grid_spec = pltpu.PrefetchScalarGridSpec(
        num_scalar_prefetch=1,
        grid=(b, n_pairs, s // rows),
        in_specs=[pl.BlockSpec((1, rows, PAIR_W), lambda bi, p, i, sh: (bi, i, p)),
                  pl.BlockSpec((1, s, PAIR_W), lambda bi, p, i, sh: (bi, 0, p)),
                  pl.BlockSpec((1, s, LANES), lambda bi, p, i, sh: (bi, 0, p))],
        out_specs=pl.BlockSpec((1, rows, LANES), lambda bi, p, i, sh: (bi, i, p)),
        scratch_shapes=[pltpu.VMEM((2, tq, PAIR_W), BF16),
                        pltpu.VMEM((2, tq, 1), F32),
                        pltpu.VMEM((2, tq, 2 * LANES), F32)],
    )
    return pl.pallas_call(
        functools.partial(_mla_attn_body, tq=tq),
        grid_spec=grid_spec,
        out_shape=jax.ShapeDtypeStruct((b, s, N_HEADS * V_DIM), BF16),
        compiler_params=_params(("parallel", "parallel", "arbitrary")),
        name="mla_attention",
    )(shift, qcat, kcat, v)


def _group_rms(x, gmat, group, g):
    chunks = []
    for c in range(x.shape[1] // 256):
        xc = x[:, c * 256:(c + 1) * 256]
        ss = _dot_hilo(xc * xc, gmat)
        chunks.append(xc * lax.rsqrt(ss * (1.0 / group) + EPS))
    y = chunks[0] if len(chunks) == 1 else jnp.concatenate(chunks, axis=-1)
    return y * g


def _rope128(x, cos, sin):
    lane = lax.broadcasted_iota(I32, (1, LANES), 1)
    first_half = (lane % ROPE_DIM) < (ROPE_DIM // 2)
    rot = jnp.where(first_half,
                    -pltpu.roll(x, LANES - ROPE_DIM // 2, 1),
                    pltpu.roll(x, ROPE_DIM // 2, 1))
    return x * cos + rot * sin


def _kv_prep_body(kva_ref, g_a_ref, g_pe_ref, cos_ref, sin_ref, wn_ref, wv_ref, g_n_ref,
                  g64_ref, k_ref, v_ref):
    lane = lax.broadcasted_iota(I32, (1, LANES), 1)
    kva = kva_ref[...]
    c = kva[:, :KV_RANK]
    cn = (c * lax.rsqrt(jnp.mean(c * c, axis=-1, keepdims=True) + EPS) * g_a_ref[...]).astype(BF16)
    pe = kva[:, KV_RANK:KV_RANK + LANES]
    ms = jnp.sum(pe * pe, axis=-1, keepdims=True) * (1.0 / ROPE_DIM)
    pen = pe * lax.rsqrt(ms + EPS) * g_pe_ref[...]
    kpe = _rope128(pen, cos_ref[...], sin_ref[...])
    kpe = jnp.where(lane < ROPE_DIM, kpe, 0.0)
    kpe2 = kpe + pltpu.roll(kpe, ROPE_DIM, 1)
    ones_lanes = jnp.logical_or(lane == ONES_LANE - LANES, lane == ONES_LANE + 1 - LANES)
    kpe2 = jnp.where(ones_lanes, 1.0, kpe2).astype(BF16)
    kn = _group_rms(_dot(cn, wn_ref[...]), g64_ref[...], NOPE_DIM, g_n_ref[...]).astype(BF16)
    v_ref[...] = _dot(cn, wv_ref[...]).astype(BF16)
    for p in range(N_HEADS // 2):
        k_ref[:, p * PAIR_W:p * PAIR_W + LANES] = kn[:, p * LANES:(p + 1) * LANES]
        k_ref[:, p * PAIR_W + LANES:(p + 1) * PAIR_W] = kpe2


def _q_prep_body(qa_ref, g_a_ref, wn_ref, wp_ref, g_n_ref, g_p_ref, cos_ref, sin_ref,
                 g64_ref, g32_ref, shift_ref, q_ref, *, scale):
    lane = lax.broadcasted_iota(I32, (1, LANES), 1)
    shift_lanes = jnp.logical_or(lane == ONES_LANE - LANES, lane == ONES_LANE + 1 - LANES)
    neg_shift = -shift_ref[...]
    qa = qa_ref[...]
    qan = (qa * lax.rsqrt(jnp.mean(qa * qa, axis=-1, keepdims=True) + EPS) * g_a_ref[...]).astype(BF16)
    qn = _group_rms(_dot(qan, wn_ref[...]), g64_ref[...], NOPE_DIM, g_n_ref[...]) * scale
    qp = _group_rms(_dot(qan, wp_ref[...]), g32_ref[...], ROPE_DIM, g_p_ref[...])
    cos = cos_ref[...]
    sin = sin_ref[...]
    for c in range(N_HEADS * ROPE_DIM // LANES):
        pe4 = _rope128(qp[:, c * LANES:(c + 1) * LANES], cos, sin) * scale
        for half in range(2):
            p = 2 * c + half
            x = pe4 if half == 0 else pltpu.roll(pe4, 2 * ROPE_DIM, 1)
            q_ref[:, p * PAIR_W:p * PAIR_W + LANES] = qn[:, p * LANES:(p + 1) * LANES].astype(BF16)
            x = jnp.where(lane < 2 * ROPE_DIM, x, jnp.where(shift_lanes, neg_shift, 0.0))
            q_ref[:, p * PAIR_W + LANES:(p + 1) * PAIR_W] = x.astype(BF16)


def _block_diag_ones(n, group):
    a = lax.broadcasted_iota(I32, (n, n), 0) // group
    b = lax.broadcasted_iota(I32, (n, n), 1) // group
    return (a == b).astype(BF16)


def _full(shape):
    return pl.BlockSpec(shape, lambda i: (0,) * len(shape))


def kv_prep(kva, g_a, g_pe_pad, cos_t, sin_t, wn, wv, g_n_t, tm, seq):
    t = kva.shape[0]
    nseq = seq // tm
    return pl.pallas_call(
        _kv_prep_body,
        grid=(t // tm,),
        in_specs=[pl.BlockSpec((tm, kva.shape[1]), lambda i: (i, 0)),
                  _full((1, KV_RANK)), _full((1, LANES)),
                  pl.BlockSpec((tm, LANES), lambda i: (i % nseq, 0)),
                  pl.BlockSpec((tm, LANES), lambda i: (i % nseq, 0)),
                  _full(wn.shape), _full(wv.shape), _full((1, wn.shape[1])),
                  _full((256, 256))],
        out_specs=[pl.BlockSpec((tm, (N_HEADS // 2) * PAIR_W), lambda i: (i, 0)),
                   pl.BlockSpec((tm, N_HEADS * V_DIM), lambda i: (i, 0))],
        out_shape=[jax.ShapeDtypeStruct((t, (N_HEADS // 2) * PAIR_W), BF16),
                   jax.ShapeDtypeStruct((t, N_HEADS * V_DIM), BF16)],
        compiler_params=_params(("parallel",)),
        name="kv_prep",
    )(kva, g_a, g_pe_pad, cos_t, sin_t, wn, wv, g_n_t, _block_diag_ones(256, NOPE_DIM))


def q_prep(qa, g_a, wn, wp, g_n_t, g_p_t, cos_t, sin_t, shift, tm, seq, scale):
    t = qa.shape[0]
    nseq = seq // tm
    return pl.pallas_call(
        functools.partial(_q_prep_body, scale=scale),
        grid=(t // tm,),
        in_specs=[pl.BlockSpec((tm, qa.shape[1]), lambda i: (i, 0)),
                  _full((1, qa.shape[1])), _full(wn.shape), _full(wp.shape),
                  _full((1, wn.shape[1])), _full((1, wp.shape[1])),
                  pl.BlockSpec((tm, LANES), lambda i: (i % nseq, 0)),
                  pl.BlockSpec((tm, LANES), lambda i: (i % nseq, 0)),
                  _full((256, 256)), _full((256, 256)), _full((1, 1))],
        out_specs=pl.BlockSpec((tm, (N_HEADS // 2) * PAIR_W), lambda i: (i, 0)),
        out_shape=jax.ShapeDtypeStruct((t, (N_HEADS // 2) * PAIR_W), BF16),
        compiler_params=_params(("parallel",)),
        name="q_prep",
    )(qa, g_a, wn, wp, g_n_t, g_p_t, cos_t, sin_t,
      _block_diag_ones(256, NOPE_DIM), _block_diag_ones(256, ROPE_DIM), shift.reshape(1, 1))


def _ffn_body(att_ref, wo_ref, res_ref, g_ref, wg_ref, wu_ref, wd_ref, o_ref, h_ref):
    x = res_ref[...] + _dot(att_ref[...], wo_ref[...])
    inv = lax.rsqrt(jnp.mean(x * x, axis=-1, keepdims=True) + EPS)
    xn = (x * inv * g_ref[...]).astype(BF16)
    _swiglu_hidden(xn, wg_ref, wu_ref, h_ref, ())
    o_ref[...] = x + _dot(h_ref[...], wd_ref[...])


def dense_ffn(att, wo, res, g, wg, wu, wd, layer, tm):
    t, d = res.shape
    f = wg.shape[2]
    once = pl.Buffered(1)
    return pl.pallas_call(
        _ffn_body,
        grid=(t // tm,),
        in_specs=[pl.BlockSpec((tm, att.shape[1]), lambda i: (i, 0)),
                  pl.BlockSpec(wo.shape, lambda i: (0, 0), pipeline_mode=once),
                  pl.BlockSpec((tm, d), lambda i: (i, 0)),
                  pl.BlockSpec((1, d), lambda i: (0, 0)),
                  pl.BlockSpec((None, d, f), lambda i: (layer, 0, 0), pipeline_mode=once),
                  pl.BlockSpec((None, d, f), lambda i: (layer, 0, 0), pipeline_mode=once),
                  pl.BlockSpec((None, f, d), lambda i: (layer, 0, 0), pipeline_mode=once)],
        out_specs=pl.BlockSpec((tm, d), lambda i: (i, 0)),
        out_shape=jax.ShapeDtypeStruct((t, d), F32),
        scratch_shapes=[pltpu.VMEM((tm, f), BF16)],
        compiler_params=_params(("parallel",)),
        name="dense_ffn",
    )(att, wo, res, g.reshape(1, d), wg, wu, wd)


def _router_body(att_ref, wo_ref, res_ref, g_ref, wr_ref, b_ref, upper_ref,
                 x_ref, xn_ref, idx_ref, gate_ref, rank_ref, cnt_ref, base_ref):
    i = pl.program_id(0)

    @pl.when(i == 0)
    def _():
        base_ref[...] = jnp.zeros_like(base_ref)

    x = res_ref[...] + _dot(att_ref[...], wo_ref[...])
    x_ref[...] = x
    xn = x * lax.rsqrt(jnp.mean(x * x, axis=-1, keepdims=True) + EPS) * g_ref[...]
    xn_ref[...] = xn
    tm = x.shape[0]

    x_hi, x_lo = _split_bf16(xn)
    w_hi, w_lo = _split_bf16(wr_ref[...])
    logits = _dot_nt(w_hi, x_hi) + _dot_nt(w_hi, x_lo) + _dot_nt(w_lo, x_hi) + b_ref[...]

    e_iota = lax.broadcasted_iota(I32, (N_EXPERTS, tm), 0)
    m1 = jnp.max(logits, axis=0, keepdims=True)
    i1 = jnp.min(jnp.where(logits == m1, e_iota, N_EXPERTS), axis=0, keepdims=True)
    sel1 = e_iota == i1
    rest = jnp.where(sel1, -jnp.inf, logits)
    m2 = jnp.max(rest, axis=0, keepdims=True)
    i2 = jnp.min(jnp.where(rest == m2, e_iota, N_EXPERTS), axis=0, keepdims=True)
    sel2 = e_iota == i2
    e2 = jnp.exp(m2 - m1)
    g1 = 1.0 / (1.0 + e2)
    idx_ref[...] = jnp.concatenate([i1, i2], axis=0)
    gate_ref[...] = jnp.concatenate([g1, e2 * g1], axis=0)

    member = jnp.logical_or(sel1, sel2)
    prefix = _dot(member.astype(BF16), upper_ref[...])
    rank = prefix + base_ref[...]
    r1 = jnp.sum(jnp.where(sel1, rank, 0.0), axis=0, keepdims=True)
    r2 = jnp.sum(jnp.where(sel2, rank, 0.0), axis=0, keepdims=True)
    rank_ref[...] = jnp.concatenate([r1, r2], axis=0).astype(I32)
    base_ref[...] += jnp.sum(member.astype(F32), axis=1, keepdims=True)
    cnt_ref[...] = jnp.broadcast_to(base_ref[...], cnt_ref.shape).astype(I32)


def moe_router(att, wo, res, g, w_router, b_router, tm):
    t, d = res.shape
    a = lax.broadcasted_iota(I32, (tm, tm), 0)
    b = lax.broadcasted_iota(I32, (tm, tm), 1)
    upper = (a < b).astype(BF16)
    return pl.pallas_call(
        _router_body,
        grid=(t // tm,),
        in_specs=[pl.BlockSpec((tm, att.shape[1]), lambda i: (i, 0)), _full(wo.shape),
                  pl.BlockSpec((tm, d), lambda i: (i, 0)),
                  _full((1, d)), _full((N_EXPERTS, d)), _full((N_EXPERTS, 1)), _full((tm, tm))],
        out_specs=[pl.BlockSpec((tm, d), lambda i: (i, 0)),
                   pl.BlockSpec((tm, d), lambda i: (i, 0)),
                   pl.BlockSpec((2, tm), lambda i: (0, i)),
                   pl.BlockSpec((2, tm), lambda i: (0, i)),
                   pl.BlockSpec((2, tm), lambda i: (0, i)),
                   _full((N_EXPERTS, LANES))],
        out_shape=[jax.ShapeDtypeStruct((t, d), F32),
                   jax.ShapeDtypeStruct((t, d), F32),
                   jax.ShapeDtypeStruct((2, t), I32),
                   jax.ShapeDtypeStruct((2, t), F32),
                   jax.ShapeDtypeStruct((2, t), I32),
                   jax.ShapeDtypeStruct((N_EXPERTS, LANES), I32)],
        scratch_shapes=[pltpu.VMEM((N_EXPERTS, 1), F32)],
        compiler_params=_params(("arbitrary",)),
        name="moe_router",
    )(att, wo, res, g.reshape(1, d), w_router.T, b_router.reshape(N_EXPERTS, 1), upper)


def _row_copy(src, src_row, dst, dst_row, sem):
    return pltpu.make_async_copy(src.at[pl.ds(src_row, 1)], dst.at[pl.ds(dst_row, 1)], sem)


def _dispatch_body(dest_ref, xn_ref, buf_in_hbm, buf_hbm, sem, *, tb, t_total):
    del buf_in_hbm
    base = pl.program_id(0) * tb

    def issue(t, c):
        for k in range(2):
            _row_copy(xn_ref, t, buf_hbm, dest_ref[k * t_total + base + t], sem).start()
        return c

    lax.fori_loop(0, tb, issue, 0, unroll=8)

    def drain(t, c):
        for k in range(2):
            _row_copy(xn_ref, 0, buf_hbm, 0, sem).wait()
        return c

    lax.fori_loop(0, tb, drain, 0, unroll=8)


def moe_dispatch(dest_flat, xn, n_rows, tb):
    t, d = xn.shape
    buf0 = jnp.zeros((n_rows, d), xn.dtype)
    grid_spec = pltpu.PrefetchScalarGridSpec(
        num_scalar_prefetch=1,
        grid=(t // tb,),
        in_specs=[pl.BlockSpec((tb, d), lambda i, dest: (i, 0)), pl.BlockSpec(memory_space=pl.ANY)],
        out_specs=pl.BlockSpec(memory_space=pl.ANY),
        scratch_shapes=[pltpu.SemaphoreType.DMA(())],
    )
    return pl.pallas_call(
        functools.partial(_dispatch_body, tb=tb, t_total=t),
        grid_spec=grid_spec,
        out_shape=jax.ShapeDtypeStruct((n_rows, d), xn.dtype),
        input_output_aliases={2: 0},
        compiler_params=_params(("arbitrary",)),
        name="moe_dispatch",
    )(dest_flat, xn, buf0)


def _swiglu_hidden(xb, wg_ref, wu_ref, h_ref, widx):
    f = h_ref.shape[1]
    for c0 in range(0, f, MXU_TILE):
        c1 = min(c0 + MXU_TILE, f)
        gate = _dot(xb, wg_ref[widx + (slice(None), slice(c0, c1))])
        up = _dot(xb, wu_ref[widx + (slice(None), slice(c0, c1))])
        h_ref[:, c0:c1] = (gate * jax.nn.sigmoid(gate) * up).astype(BF16)


def _experts_body(be_ref, nv_ref, x_ref, wg_ref, wu_ref, wd_ref, o_ref, h_ref):
    del be_ref
    valid = pl.program_id(0) < nv_ref[0]

    @pl.when(valid)
    def _():
        _swiglu_hidden(x_ref[...].astype(BF16), wg_ref, wu_ref, h_ref, (0,))
        o_ref[...] = _dot(h_ref[...], wd_ref[0])

    @pl.when(jnp.logical_not(valid))
    def _():
        o_ref[...] = jnp.zeros_like(o_ref)


def moe_experts(block_e, n_valid, buf, wg, wu, wd, layer, tm):
    n_rows, d = buf.shape
    f = wg.shape[3]
    once = pl.Buffered(1)
    grid_spec = pltpu.PrefetchScalarGridSpec(
        num_scalar_prefetch=2,
        grid=(n_rows // tm,),
        in_specs=[pl.BlockSpec((tm, d), lambda i, be, nv: (i, 0)),
                  pl.BlockSpec((None, 1, d, f), lambda i, be, nv: (layer, be[i], 0, 0), pipeline_mode=once),
                  pl.BlockSpec((None, 1, d, f), lambda i, be, nv: (layer, be[i], 0, 0), pipeline_mode=once),
                  pl.BlockSpec((None, 1, f, d), lambda i, be, nv: (layer, be[i], 0, 0), pipeline_mode=once)],
        out_specs=pl.BlockSpec((tm, d), lambda i, be, nv: (i, 0)),
        scratch_shapes=[pltpu.VMEM((tm, f), BF16)],
    )
    return pl.pallas_call(
        _experts_body,
        grid_spec=grid_spec,
        out_shape=jax.ShapeDtypeStruct((n_rows, d), F32),
        compiler_params=_params(("arbitrary",), vmem=VMEM_LIMIT_BIG),
        name="moe_experts",
    )(block_e, n_valid, buf, wg, wu, wd)


def _combine_body(dest_ref, x_ref, gate_ref, eo_hbm, o_ref, buf_ref, sems, *, tb, t_total):
    i = pl.program_id(0)
    slot = i % 2

    def start_block(block, to_slot):
        base = block * tb

        def issue(t, c):
            for k in range(2):
                _row_copy(eo_hbm, dest_ref[k * t_total + base + t], buf_ref.at[to_slot, k], t,
                          sems.at[to_slot]).start()
            return c

        lax.fori_loop(0, tb, issue, 0, unroll=8)

    @pl.when(i == 0)
    def _():
        start_block(0, 0)

    @pl.when(i + 1 < pl.num_programs(0))
    def _():
        start_block(i + 1, 1 - slot)

    def drain(t, c):
        for k in range(2):
            _row_copy(eo_hbm, 0, buf_ref.at[slot, k], 0, sems.at[slot]).wait()
        return c

    lax.fori_loop(0, tb, drain, 0, unroll=8)
    gates = gate_ref[...]
    o_ref[...] = x_ref[...] + gates[:, 0:1] * buf_ref[slot, 0] + gates[:, 1:2] * buf_ref[slot, 1]


def moe_combine(dest_flat, x, gates_t, expert_out, tb):
    t, d = x.shape
    grid_spec = pltpu.PrefetchScalarGridSpec(
        num_scalar_prefetch=1,
        grid=(t // tb,),
        in_specs=[pl.BlockSpec((tb, d), lambda i, dest: (i, 0)),
                  pl.BlockSpec((tb, 2), lambda i, dest: (i, 0)),
                  pl.BlockSpec(memory_space=pl.ANY)],
        out_specs=pl.BlockSpec((tb, d), lambda i, dest: (i, 0)),
        scratch_shapes=[pltpu.VMEM((2, 2, tb, d), F32), pltpu.SemaphoreType.DMA((2,))],
    )
    return pl.pallas_call(
        functools.partial(_combine_body, tb=tb, t_total=t),
        grid_spec=grid_spec,
        out_shape=jax.ShapeDtypeStruct((t, d), F32),
        compiler_params=_params(("arbitrary",)),
        name="moe_combine",
    )(dest_flat, x, gates_t, expert_out)


def moe_layer(att, wo, res, g, w_router, b_router, wg, wu, wd, layer, tiles):
    t, d = res.shape
    tm = tiles["moe_tm"]
    x, xn, idx, gates, rank, cnt = moe_router(att, wo, res, g, w_router, b_router, tiles["router_tm"])
    counts = cnt[:, 0]
    padded = ((counts + tm - 1) // tm) * tm
    pends = jnp.cumsum(padded)
    pstarts = pends - padded
    group_start = sum(jnp.where(idx == e, pstarts[e], 0) for e in range(N_EXPERTS))
    dest_flat = (group_start + rank).reshape(-1)
    n_rows = 2 * t + N_EXPERTS * tm
    n_blocks = n_rows // tm
    block_row = jnp.arange(n_blocks, dtype=I32) * tm
    block_e = jnp.minimum(sum((block_row >= pends[e]).astype(I32) for e in range(N_EXPERTS)),
                          N_EXPERTS - 1)
    n_valid = (pends[-1:] // tm).astype(I32)
    buf = moe_dispatch(dest_flat, xn, n_rows, tiles["moe_tb"])
    eo = moe_experts(block_e, n_valid, buf, wg, wu, wd, layer, tm)
    return moe_combine(dest_flat, x, gates.T, eo, tiles["moe_tb"])


def _tiles(t, s):
    def fit(n, want):
        while n % want:
            want //= 2
        return want
    return dict(
        proj_tm=fit(t, 1024), attn_tq=fit(s, 256), mla_tq=fit(s, 512), prep_tm=fit(s, 512),
        ffn_tm=fit(t, 512), router_tm=fit(t, 512), moe_tm=fit(t, 512), moe_tb=fit(t, 512),
    )


def _mla_score_bound(g_q_nope, g_q_pe, g_k_nope, g_k_pe, scale):
    q2 = NOPE_DIM * jnp.max(g_q_nope * g_q_nope) + ROPE_DIM * jnp.max(g_q_pe * g_q_pe)
    k2 = NOPE_DIM * jnp.max(g_k_nope * g_k_nope) + ROPE_DIM * jnp.max(g_k_pe * g_k_pe)
    return (scale * jnp.sqrt(q2 * k2) * 1.02 + 1e-2).reshape(1)


def _rope_tables(seq):
    pos = jnp.arange(seq, dtype=F32)
    inv_freq = ROPE_THETA ** (-jnp.arange(0, ROPE_DIM, 2, dtype=F32) / ROPE_DIM)
    ang = pos[:, None] * inv_freq[None, :]
    ang = jnp.concatenate([ang, ang], axis=-1)
    reps = LANES // ROPE_DIM
    return jnp.tile(jnp.cos(ang), (1, reps)), jnp.tile(jnp.sin(ang), (1, reps))


def kernel(x, g_mix, g_ffn, sb_w_qkv, sb_w_o, kv_g_src, kv_w_a, kv_g_a, kv_w_b, kv_g_k_nope, kv_g_k_pe, mla_w_q_a, mla_g_q_a, mla_w_q_b, mla_g_q_nope, mla_g_q_pe, mla_w_o, ffn_w_gate, ffn_w_up, ffn_w_down, moe_w_router, moe_b_router, moe_w_gate, moe_w_up, moe_w_down):
    b, s, d = x.shape
    t = b * s
    depth = g_mix.shape[0]
    n_a = sb_w_qkv.shape[0]
    tiles = _tiles(t, s)
    cos_t, sin_t = _rope_tables(s)
    h = x.reshape(t, d)
    hw = N_HEADS * SB_HEAD_DIM
    kcat = vcat = None
    ffn_w = [w.astype(BF16) for w in (ffn_w_gate, ffn_w_up, ffn_w_down)]
    moe_w = [w.astype(BF16) for w in (moe_w_gate, moe_w_up, moe_w_down)]

    for i in range(depth):
        if i < n_a:
            col_scale = jnp.concatenate([jnp.full((hw,), LOG2E / math.sqrt(SB_HEAD_DIM), F32),
                                         jnp.ones((2 * hw,), F32)])
            w_qkv = (sb_w_qkv[i] * col_scale).astype(BF16)
            qkv = rms_matmul(h, g_mix[i], w_qkv, BF16, tiles["ffn_tm"], w_qkv.shape[1])
            att = sb_attention(qkv.reshape(b, s, 3 * hw), tiles["attn_tq"])
            w_o = sb_w_o[i].astype(BF16)
        else:
            j = i - n_a
            if kcat is None:
                pad = jnp.zeros((d, LANES - ROPE_DIM), F32)
                w_a = jnp.concatenate([kv_w_a, pad], axis=1).astype(BF16)
                kva = rms_matmul(h, kv_g_src, w_a, F32, tiles["proj_tm"], w_a.shape[1])
                w_b = kv_w_b.reshape(KV_RANK, N_HEADS, NOPE_DIM + V_DIM)
                wn = w_b[:, :, :NOPE_DIM].reshape(KV_RANK, N_HEADS * NOPE_DIM).astype(BF16)
                wv = w_b[:, :, NOPE_DIM:].reshape(KV_RANK, N_HEADS * V_DIM).astype(BF16)
                g_pe_pad = jnp.concatenate([kv_g_k_pe, jnp.zeros((LANES - ROPE_DIM,), F32)]).reshape(1, LANES)
                kcat, vcat = kv_prep(kva, kv_g_a.reshape(1, KV_RANK), g_pe_pad, cos_t, sin_t, wn, wv,
                                     jnp.tile(kv_g_k_nope, N_HEADS).reshape(1, -1), tiles["prep_tm"], s)
                kcat = kcat.reshape(b, s, -1)
                vcat = vcat.reshape(b, s, -1)
            qa = rms_matmul(h, g_mix[i], mla_w_q_a[j].astype(BF16), F32, tiles["proj_tm"], mla_w_q_a.shape[2])
            w_qb = mla_w_q_b[j].reshape(-1, N_HEADS, NOPE_DIM + ROPE_DIM)
            wqn = w_qb[:, :, :NOPE_DIM].reshape(-1, N_HEADS * NOPE_DIM).astype(BF16)
            wqp = w_qb[:, :, NOPE_DIM:].reshape(-1, N_HEADS * ROPE_DIM).astype(BF16)
            scale = LOG2E / math.sqrt(NOPE_DIM + ROPE_DIM)
            shift = _mla_score_bound(mla_g_q_nope[j], mla_g_q_pe[j], kv_g_k_nope, kv_g_k_pe, scale)
            qcat = q_prep(qa, mla_g_q_a[j].reshape(1, -1), wqn, wqp,
                          jnp.tile(mla_g_q_nope[j], N_HEADS).reshape(1, -1),
                          jnp.tile(mla_g_q_pe[j], N_HEADS).reshape(1, -1),
                          cos_t, sin_t, shift, tiles["prep_tm"], s, scale)
            att = mla_attention(shift, qcat.reshape(b, s, -1), kcat, vcat, tiles["mla_tq"])
            w_o = mla_w_o[j].astype(BF16)

        att = att.reshape(t, -1)
        m = i // 2
        if i % 2 == 0:
            h = dense_ffn(att, w_o, h, g_ffn[i], *ffn_w, m, tiles["ffn_tm"])
        else:
            h = moe_layer(att, w_o, h, g_ffn[i], moe_w_router[m], moe_b_router[m], *moe_w, m, tiles)
    return h.reshape(b, s, d)
```

```python
import functools
import math

import jax
import jax.numpy as jnp
from jax import lax
from jax.experimental import pallas as pl
from jax.experimental.pallas import tpu as pltpu

F32 = jnp.float32
BF16 = jnp.bfloat16
I32 = jnp.int32

EPS = 1e-6
N_HEADS = 16
SB_HEAD_DIM = 64
NOPE_DIM = 64
ROPE_DIM = 32
V_DIM = 64
KV_RANK = 256
ROPE_THETA = 10000.0
N_EXPERTS = 8
LANES = 128
PAIR_W = 2 * NOPE_DIM + LANES
ONES_LANE = 2 * NOPE_DIM + 2 * ROPE_DIM
MLA_STATIC_SHIFT_MAX = 50.0
MLA_GROUP = 4
LOG2E = 1.4426950408889634
SB_SKIP_EXP = 110.0
SB_SUBS = 8
SB_MASKED_LOGIT = -1e30
MXU_TILE = 256
VMEM_LIMIT = 52 * 1024 * 1024
VMEM_LIMIT_BIG = 58 * 1024 * 1024


def _params(sem, vmem=VMEM_LIMIT):
    return pltpu.CompilerParams(dimension_semantics=sem, vmem_limit_bytes=vmem)


def _dot(a, b):
    return jnp.dot(a, b, preferred_element_type=F32)


def _dot_nt(a, b):
    return lax.dot_general(a, b, (((1,), (1,)), ((), ())), preferred_element_type=F32)


def _split_bf16(x):
    hi = x.astype(BF16)
    lo = (x - hi.astype(F32)).astype(BF16)
    return hi, lo


def _dot_hilo(x, w):
    hi, lo = _split_bf16(x)
    return _dot(hi, w) + _dot(lo, w)


def _rms_matmul_body(x_ref, g_ref, w_ref, o_ref, xn_ref):
    @pl.when(pl.program_id(1) == 0)
    def _():
        x = x_ref[...]
        inv = lax.rsqrt(jnp.mean(x * x, axis=-1, keepdims=True) + EPS)
        xn_ref[...] = (x * inv * g_ref[...]).astype(BF16)

    o_ref[...] = _dot(xn_ref[...], w_ref[...]).astype(o_ref.dtype)


def rms_matmul(x, g, w, out_dtype, tm, tn):
    t, d = x.shape
    n = w.shape[1]
    w_mode = pl.Buffered(1) if tn == n else None
    return pl.pallas_call(
        _rms_matmul_body,
        grid=(t // tm, n // tn),
        in_specs=[pl.BlockSpec((tm, d), lambda i, j: (i, 0)),
                  pl.BlockSpec((1, d), lambda i, j: (0, 0)),
                  pl.BlockSpec((d, tn), lambda i, j: (0, j), pipeline_mode=w_mode)],
        out_specs=pl.BlockSpec((tm, tn), lambda i, j: (i, j)),
        out_shape=jax.ShapeDtypeStruct((t, n), out_dtype),
        scratch_shapes=[pltpu.VMEM((tm, d), BF16)],
        compiler_params=_params(("parallel", "arbitrary")),
        name="rms_matmul",
    )(x, g.reshape(1, d), w)


def _softplus2(z):
    return jnp.maximum(jnp.log2(1.0 + jnp.exp2(jnp.minimum(z, 64.0))), z)


def _key_norm_max(k_ref, head_masks, kmax_ref):
    k = k_ref[0].astype(F32)
    k2 = k * k
    for hh, mask in enumerate(head_masks):
        n2 = jnp.sum(jnp.where(mask, k2, 0.0), axis=-1, keepdims=True)
        kmax_ref[hh] = jnp.max(n2, axis=0, keepdims=True)


def _logit_bound(qm, kmax):
    qf = qm.astype(F32)
    return jnp.sqrt(jnp.sum(qf * qf, axis=-1, keepdims=True) * kmax) * 1.01 + 1e-2


def _sb_attn_body(q_ref, k_ref, v_ref, tri_ref, o_ref, kmax_ref, qs_ref, zb_ref, carry_ref, acc_ref,
                  *, tq):
    step = pl.program_id(2)
    lane = lax.broadcasted_iota(I32, (1, LANES), 1)
    head_masks = [(lane // SB_HEAD_DIM) == hh for hh in range(2)]

    @pl.when(step == 0)
    def _():
        _key_norm_max(k_ref, head_masks, kmax_ref)

    for sub in range(SB_SUBS):
        q2 = q_ref[0, sub * tq:(sub + 1) * tq, :]
        for hh in range(2):
            qm = jnp.where(head_masks[hh], q2, jnp.zeros_like(q2))
            qs_ref[sub, hh * tq:(hh + 1) * tq, :] = qm
            zb_ref[sub, hh * tq:(hh + 1) * tq, :] = _logit_bound(qm, kmax_ref[hh])

    row = lax.broadcasted_iota(I32, (2 * tq, tq), 0) % tq
    col = lax.broadcasted_iota(I32, (2 * tq, tq), 1)
    causal = col < row

    def chain(c, start, masked):
        z = _dot_nt(qs_ref[c], k_ref[0, pl.ds(start, tq), :])
        if masked:
            z = jnp.where(causal, z, SB_MASKED_LOGIT)
        sp = _softplus2(z)
        csum = _dot(sp.astype(BF16), tri_ref[...])
        e = jnp.exp2(z - csum)
        return _dot(e.astype(BF16), v_ref[0, pl.ds(start, tq), :]), csum[:, 0:1]

    for sub in range(SB_SUBS):
        qi = SB_SUBS * step + sub
        has_prev = qi >= 1
        prev_start = pl.multiple_of(jnp.maximum(qi - 1, 0) * tq, tq)
        pv_d, tot_d = chain(sub, pl.multiple_of(qi * tq, tq), True)
        pv_p, tot_p = chain(sub, prev_start, False)
        acc_ref[sub] = pv_d + jnp.where(has_prev, jnp.exp2(-tot_d), 0.0) * pv_p
        carry_ref[sub] = tot_d + jnp.where(has_prev, tot_p, 0.0)

    for sub in range(SB_SUBS):
        qi = SB_SUBS * step + sub

        def cond(jj):
            live = jnp.max(zb_ref[sub] - carry_ref[sub]) > -SB_SKIP_EXP * LOG2E
            return jnp.logical_and(jj <= qi, live)

        def body(jj):
            pv, tot = chain(sub, pl.multiple_of((qi - jj) * tq, tq), False)
            carry = carry_ref[sub]
            acc_ref[sub] += jnp.exp2(-carry) * pv
            carry_ref[sub] = carry + tot
            return jj + 1

        lax.while_loop(cond, body, jnp.int32(2))
        acc = acc_ref[sub]
        o_ref[0, sub * tq:(sub + 1) * tq, :] = jnp.where(
            head_masks[0], acc[:tq], acc[tq:]).astype(o_ref.dtype)


def sb_attention(qkv, tq):
    b, s, _ = qkv.shape
    n_pairs = N_HEADS // 2
    rows = SB_SUBS * tq
    ji = lax.broadcasted_iota(I32, (tq, tq), 0)
    si = lax.broadcasted_iota(I32, (tq, tq), 1)
    tri = (ji >= si).astype(BF16)
    return pl.pallas_call(
        functools.partial(_sb_attn_body, tq=tq),
        grid=(b, n_pairs, s // rows),
        in_specs=[pl.BlockSpec((1, rows, LANES), lambda bi, p, i: (bi, i, p)),
                  pl.BlockSpec((1, s, LANES), lambda bi, p, i: (bi, 0, n_pairs + p)),
                  pl.BlockSpec((1, s, LANES), lambda bi, p, i: (bi, 0, 2 * n_pairs + p)),
                  pl.BlockSpec((tq, tq), lambda bi, p, i: (0, 0))],
        out_specs=pl.BlockSpec((1, rows, LANES), lambda bi, p, i: (bi, i, p)),
        out_shape=jax.ShapeDtypeStruct((b, s, N_HEADS * SB_HEAD_DIM), BF16),
        scratch_shapes=[pltpu.VMEM((2, 1, 1), F32),
                        pltpu.VMEM((SB_SUBS, 2 * tq, LANES), BF16),
                        pltpu.VMEM((SB_SUBS, 2 * tq, 1), F32),
                        pltpu.VMEM((SB_SUBS, 2 * tq, 1), F32),
                        pltpu.VMEM((SB_SUBS, 2 * tq, LANES), F32)],
        compiler_params=_params(("parallel", "parallel", "arbitrary")),
        name="sb_attention",
    )(qkv, qkv, qkv, tri)


def _mla_attn_body(shift_ref, q_ref, k_ref, v_ref, o_ref, qs_ref, m_ref, acc_ref, *, tq):
    qi = pl.program_id(2)
    shift_max = shift_ref[0]
    lane_q = lax.broadcasted_iota(I32, (1, PAIR_W), 1)
    lane_o = lax.broadcasted_iota(I32, (1, LANES), 1)
    q2 = q_ref[0]
    for hh in range(2):
        nope = (lane_q // NOPE_DIM) == hh
        pe = jnp.logical_and(lane_q >= 2 * NOPE_DIM, ((lane_q - 2 * NOPE_DIM) // ROPE_DIM) == hh)
        keep = jnp.logical_or(jnp.logical_or(nope, pe), lane_q == ONES_LANE + hh)
        qs_ref[hh] = jnp.where(keep, q2, jnp.zeros_like(q2))
        acc_ref[hh] = jnp.zeros((tq, 2 * LANES), F32)

    ones = jnp.ones((tq, LANES), BF16)
    causal = (lax.broadcasted_iota(I32, (tq, tq), 1) <= lax.broadcasted_iota(I32, (tq, tq), 0))

    def scores(hh, j, masked):
        sc = _dot_nt(qs_ref[hh], k_ref[0, pl.ds(pl.multiple_of(j * tq, tq), tq), :])
        return jnp.where(causal, sc, -jnp.inf) if masked else sc

    def values(j):
        return jnp.concatenate([v_ref[0, pl.ds(pl.multiple_of(j * tq, tq), tq), :], ones], axis=-1)

    def static_blocks(blocks):
        for hh in range(2):
            acc_ref[hh] += sum(_dot(jnp.exp2(scores(hh, j, masked)).astype(BF16), values(j))
                               for j, masked in blocks)

    def online_block(j, masked):
        for hh in range(2):
            sc = scores(hh, j, masked)
            m_prev = m_ref[hh]
            m_new = jnp.maximum(m_prev, jnp.max(sc, axis=-1, keepdims=True))
            p = jnp.exp2(sc - m_new)
            acc_ref[hh] = jnp.exp2(m_prev - m_new) * acc_ref[hh] + _dot(p.astype(BF16), values(j))
            m_ref[hh] = m_new

    @pl.when(shift_max <= MLA_STATIC_SHIFT_MAX)
    def _():
        def group(j, c):
            static_blocks([(MLA_GROUP * j + u, False) for u in range(MLA_GROUP)])
            return c
        lax.fori_loop(0, qi // MLA_GROUP, group, 0)

        for rest in range(MLA_GROUP):
            @pl.when(qi % MLA_GROUP == rest)
            def _():
                static_blocks([(qi - rest + u, False) for u in range(rest)] + [(qi, True)])

    @pl.when(shift_max > MLA_STATIC_SHIFT_MAX)
    def _():
        m_ref[...] = jnp.full_like(m_ref, -jnp.inf)

        def body(j, c):
            online_block(j, False)
            return c
        lax.fori_loop(0, qi, body, 0)
        online_block(qi, True)

    outs = [acc_ref[hh][:, :LANES] / acc_ref[hh][:, LANES:] for hh in range(2)]
    o_ref[0] = jnp.where(lane_o < V_DIM, outs[0], outs[1]).astype(o_ref.dtype)


def mla_attention(shift, qcat, kcat, v, tq):
    b, s, _ = qcat.shape
    n_pairs = N_HEADS // 2
    grid_spec = pltpu.PrefetchScalarGridSpec(
        num_scalar_prefetch=1,
        grid=(b, n_pairs, s // tq),
        in_specs=[pl.BlockSpec((1, tq, PAIR_W), lambda bi, p, i, sh: (bi, i, p)),
                  pl.BlockSpec((1, s, PAIR_W), lambda bi, p, i, sh: (bi, 0, p)),
                  pl.BlockSpec((1, s, LANES), lambda bi, p, i, sh: (bi, 0, p))],
        out_specs=pl.BlockSpec((1, tq, LANES), lambda bi, p, i, sh: (bi, i, p)),
        scratch_shapes=[pltpu.VMEM((2, tq, PAIR_W), BF16),
                        pltpu.VMEM((2, tq, 1), F32),
                        pltpu.VMEM((2, tq, 2 * LANES), F32)],
    )
    return pl.pallas_call(
        functools.partial(_mla_attn_body, tq=tq),
        grid_spec=grid_spec,
        out_shape=jax.ShapeDtypeStruct((b, s, N_HEADS * V_DIM), BF16),
        compiler_params=_params(("parallel", "parallel", "arbitrary")),
        name="mla_attention",
    )(shift, qcat, kcat, v)


def _group_rms(x, gmat, group, g):
    chunks = []
    for c in range(x.shape[1] // 256):
        xc = x[:, c * 256:(c + 1) * 256]
        ss = _dot_hilo(xc * xc, gmat)
        chunks.append(xc * lax.rsqrt(ss * (1.0 / group) + EPS))
    y = chunks[0] if len(chunks) == 1 else jnp.concatenate(chunks, axis=-1)
    return y * g


def _rope128(x, cos, sin):
    lane = lax.broadcasted_iota(I32, (1, LANES), 1)
    first_half = (lane % ROPE_DIM) < (ROPE_DIM // 2)
    rot = jnp.where(first_half,
                    -pltpu.roll(x, LANES - ROPE_DIM // 2, 1),
                    pltpu.roll(x, ROPE_DIM // 2, 1))
    return x * cos + rot * sin


def _kv_prep_body(kva_ref, g_a_ref, g_pe_ref, cos_ref, sin_ref, wn_ref, wv_ref, g_n_ref,
                  g64_ref, k_ref, v_ref):
    lane = lax.broadcasted_iota(I32, (1, LANES), 1)
    kva = kva_ref[...]
    c = kva[:, :KV_RANK]
    cn = (c * lax.rsqrt(jnp.mean(c * c, axis=-1, keepdims=True) + EPS) * g_a_ref[...]).astype(BF16)
    pe = kva[:, KV_RANK:KV_RANK + LANES]
    ms = jnp.sum(pe * pe, axis=-1, keepdims=True) * (1.0 / ROPE_DIM)
    pen = pe * lax.rsqrt(ms + EPS) * g_pe_ref[...]
    kpe = _rope128(pen, cos_ref[...], sin_ref[...])
    kpe = jnp.where(lane < ROPE_DIM, kpe, 0.0)
    kpe2 = kpe + pltpu.roll(kpe, ROPE_DIM, 1)
    ones_lanes = jnp.logical_or(lane == ONES_LANE - LANES, lane == ONES_LANE + 1 - LANES)
    kpe2 = jnp.where(ones_lanes, 1.0, kpe2).astype(BF16)
    kn = _group_rms(_dot(cn, wn_ref[...]), g64_ref[...], NOPE_DIM, g_n_ref[...]).astype(BF16)
    v_ref[...] = _dot(cn, wv_ref[...]).astype(BF16)
    for p in range(N_HEADS // 2):
        k_ref[:, p * PAIR_W:p * PAIR_W + LANES] = kn[:, p * LANES:(p + 1) * LANES]
        k_ref[:, p * PAIR_W + LANES:(p + 1) * PAIR_W] = kpe2


def _q_prep_body(qa_ref, g_a_ref, wn_ref, wp_ref, g_n_ref, g_p_ref, cos_ref, sin_ref,
                 g64_ref, g32_ref, shift_ref, q_ref, *, scale):
    lane = lax.broadcasted_iota(I32, (1, LANES), 1)
    shift_lanes = jnp.logical_or(lane == ONES_LANE - LANES, lane == ONES_LANE + 1 - LANES)
    neg_shift = -shift_ref[...]
    qa = qa_ref[...]
    qan = (qa * lax.rsqrt(jnp.mean(qa * qa, axis=-1, keepdims=True) + EPS) * g_a_ref[...]).astype(BF16)
    qn = _group_rms(_dot(qan, wn_ref[...]), g64_ref[...], NOPE_DIM, g_n_ref[...]) * scale
    qp = _group_rms(_dot(qan, wp_ref[...]), g32_ref[...], ROPE_DIM, g_p_ref[...])
    cos = cos_ref[...]
    sin = sin_ref[...]
    for c in range(N_HEADS * ROPE_DIM // LANES):
        pe4 = _rope128(qp[:, c * LANES:(c + 1) * LANES], cos, sin) * scale
        for half in range(2):
            p = 2 * c + half
            x = pe4 if half == 0 else pltpu.roll(pe4, 2 * ROPE_DIM, 1)
            q_ref[:, p * PAIR_W:p * PAIR_W + LANES] = qn[:, p * LANES:(p + 1) * LANES].astype(BF16)
            x = jnp.where(lane < 2 * ROPE_DIM, x, jnp.where(shift_lanes, neg_shift, 0.0))
            q_ref[:, p * PAIR_W + LANES:(p + 1) * PAIR_W] = x.astype(BF16)


def _block_diag_ones(n, group):
    a = lax.broadcasted_iota(I32, (n, n), 0) // group
    b = lax.broadcasted_iota(I32, (n, n), 1) // group
    return (a == b).astype(BF16)


def _full(shape):
    return pl.BlockSpec(shape, lambda i: (0,) * len(shape))


def kv_prep(kva, g_a, g_pe_pad, cos_t, sin_t, wn, wv, g_n_t, tm, seq):
    t = kva.shape[0]
    nseq = seq // tm
    return pl.pallas_call(
        _kv_prep_body,
        grid=(t // tm,),
        in_specs=[pl.BlockSpec((tm, kva.shape[1]), lambda i: (i, 0)),
                  _full((1, KV_RANK)), _full((1, LANES)),
                  pl.BlockSpec((tm, LANES), lambda i: (i % nseq, 0)),
                  pl.BlockSpec((tm, LANES), lambda i: (i % nseq, 0)),
                  _full(wn.shape), _full(wv.shape), _full((1, wn.shape[1])),
                  _full((256, 256))],
        out_specs=[pl.BlockSpec((tm, (N_HEADS // 2) * PAIR_W), lambda i: (i, 0)),
                   pl.BlockSpec((tm, N_HEADS * V_DIM), lambda i: (i, 0))],
        out_shape=[jax.ShapeDtypeStruct((t, (N_HEADS // 2) * PAIR_W), BF16),
                   jax.ShapeDtypeStruct((t, N_HEADS * V_DIM), BF16)],
        compiler_params=_params(("parallel",)),
        name="kv_prep",
    )(kva, g_a, g_pe_pad, cos_t, sin_t, wn, wv, g_n_t, _block_diag_ones(256, NOPE_DIM))


def q_prep(qa, g_a, wn, wp, g_n_t, g_p_t, cos_t, sin_t, shift, tm, seq, scale):
    t = qa.shape[0]
    nseq = seq // tm
    return pl.pallas_call(
        functools.partial(_q_prep_body, scale=scale),
        grid=(t // tm,),
        in_specs=[pl.BlockSpec((tm, qa.shape[1]), lambda i: (i, 0)),
                  _full((1, qa.shape[1])), _full(wn.shape), _full(wp.shape),
                  _full((1, wn.shape[1])), _full((1, wp.shape[1])),
                  pl.BlockSpec((tm, LANES), lambda i: (i % nseq, 0)),
                  pl.BlockSpec((tm, LANES), lambda i: (i % nseq, 0)),
                  _full((256, 256)), _full((256, 256)), _full((1, 1))],
        out_specs=pl.BlockSpec((tm, (N_HEADS // 2) * PAIR_W), lambda i: (i, 0)),
        out_shape=jax.ShapeDtypeStruct((t, (N_HEADS // 2) * PAIR_W), BF16),
        compiler_params=_params(("parallel",)),
        name="q_prep",
    )(qa, g_a, wn, wp, g_n_t, g_p_t, cos_t, sin_t,
      _block_diag_ones(256, NOPE_DIM), _block_diag_ones(256, ROPE_DIM), shift.reshape(1, 1))


def _ffn_body(att_ref, wo_ref, res_ref, g_ref, wg_ref, wu_ref, wd_ref, o_ref, h_ref):
    x = res_ref[...] + _dot(att_ref[...], wo_ref[...])
    inv = lax.rsqrt(jnp.mean(x * x, axis=-1, keepdims=True) + EPS)
    xn = (x * inv * g_ref[...]).astype(BF16)
    _swiglu_hidden(xn, wg_ref, wu_ref, h_ref, ())
    o_ref[...] = x + _dot(h_ref[...], wd_ref[...])


def dense_ffn(att, wo, res, g, wg, wu, wd, layer, tm):
    t, d = res.shape
    f = wg.shape[2]
    once = pl.Buffered(1)
    return pl.pallas_call(
        _ffn_body,
        grid=(t // tm,),
        in_specs=[pl.BlockSpec((tm, att.shape[1]), lambda i: (i, 0)),
                  pl.BlockSpec(wo.shape, lambda i: (0, 0), pipeline_mode=once),
                  pl.BlockSpec((tm, d), lambda i: (i, 0)),
                  pl.BlockSpec((1, d), lambda i: (0, 0)),
                  pl.BlockSpec((None, d, f), lambda i: (layer, 0, 0), pipeline_mode=once),
                  pl.BlockSpec((None, d, f), lambda i: (layer, 0, 0), pipeline_mode=once),
                  pl.BlockSpec((None, f, d), lambda i: (layer, 0, 0), pipeline_mode=once)],
        out_specs=pl.BlockSpec((tm, d), lambda i: (i, 0)),
        out_shape=jax.ShapeDtypeStruct((t, d), F32),
        scratch_shapes=[pltpu.VMEM((tm, f), BF16)],
        compiler_params=_params(("parallel",)),
        name="dense_ffn",
    )(att, wo, res, g.reshape(1, d), wg, wu, wd)


def _router_body(att_ref, wo_ref, res_ref, g_ref, wr_ref, b_ref, upper_ref,
                 x_ref, xn_ref, idx_ref, gate_ref, rank_ref, cnt_ref, base_ref):
    i = pl.program_id(0)

    @pl.when(i == 0)
    def _():
        base_ref[...] = jnp.zeros_like(base_ref)

    x = res_ref[...] + _dot(att_ref[...], wo_ref[...])
    x_ref[...] = x
    xn = x * lax.rsqrt(jnp.mean(x * x, axis=-1, keepdims=True) + EPS) * g_ref[...]
    xn_ref[...] = xn
    tm = x.shape[0]

    x_hi, x_lo = _split_bf16(xn)
    w_hi, w_lo = _split_bf16(wr_ref[...])
    logits = _dot_nt(w_hi, x_hi) + _dot_nt(w_hi, x_lo) + _dot_nt(w_lo, x_hi) + b_ref[...]

    e_iota = lax.broadcasted_iota(I32, (N_EXPERTS, tm), 0)
    m1 = jnp.max(logits, axis=0, keepdims=True)
    i1 = jnp.min(jnp.where(logits == m1, e_iota, N_EXPERTS), axis=0, keepdims=True)
    sel1 = e_iota == i1
    rest = jnp.where(sel1, -jnp.inf, logits)
    m2 = jnp.max(rest, axis=0, keepdims=True)
    i2 = jnp.min(jnp.where(rest == m2, e_iota, N_EXPERTS), axis=0, keepdims=True)
    sel2 = e_iota == i2
    e2 = jnp.exp(m2 - m1)
    g1 = 1.0 / (1.0 + e2)
    idx_ref[...] = jnp.concatenate([i1, i2], axis=0)
    gate_ref[...] = jnp.concatenate([g1, e2 * g1], axis=0)

    member = jnp.logical_or(sel1, sel2)
    prefix = _dot(member.astype(BF16), upper_ref[...])
    rank = prefix + base_ref[...]
    r1 = jnp.sum(jnp.where(sel1, rank, 0.0), axis=0, keepdims=True)
    r2 = jnp.sum(jnp.where(sel2, rank, 0.0), axis=0, keepdims=True)
    rank_ref[...] = jnp.concatenate([r1, r2], axis=0).astype(I32)
    base_ref[...] += jnp.sum(member.astype(F32), axis=1, keepdims=True)
    cnt_ref[...] = jnp.broadcast_to(base_ref[...], cnt_ref.shape).astype(I32)


def moe_router(att, wo, res, g, w_router, b_router, tm):
    t, d = res.shape
    a = lax.broadcasted_iota(I32, (tm, tm), 0)
    b = lax.broadcasted_iota(I32, (tm, tm), 1)
    upper = (a < b).astype(BF16)
    return pl.pallas_call(
        _router_body,
        grid=(t // tm,),
        in_specs=[pl.BlockSpec((tm, att.shape[1]), lambda i: (i, 0)), _full(wo.shape),
                  pl.BlockSpec((tm, d), lambda i: (i, 0)),
                  _full((1, d)), _full((N_EXPERTS, d)), _full((N_EXPERTS, 1)), _full((tm, tm))],
        out_specs=[pl.BlockSpec((tm, d), lambda i: (i, 0)),
                   pl.BlockSpec((tm, d), lambda i: (i, 0)),
                   pl.BlockSpec((2, tm), lambda i: (0, i)),
                   pl.BlockSpec((2, tm), lambda i: (0, i)),
                   pl.BlockSpec((2, tm), lambda i: (0, i)),
                   _full((N_EXPERTS, LANES))],
        out_shape=[jax.ShapeDtypeStruct((t, d), F32),
                   jax.ShapeDtypeStruct((t, d), F32),
                   jax.ShapeDtypeStruct((2, t), I32),
                   jax.ShapeDtypeStruct((2, t), F32),
                   jax.ShapeDtypeStruct((2, t), I32),
                   jax.ShapeDtypeStruct((N_EXPERTS, LANES), I32)],
        scratch_shapes=[pltpu.VMEM((N_EXPERTS, 1), F32)],
        compiler_params=_params(("arbitrary",)),
        name="moe_router",
    )(att, wo, res, g.reshape(1, d), w_router.T, b_router.reshape(N_EXPERTS, 1), upper)


def _row_copy(src, src_row, dst, dst_row, sem):
    return pltpu.make_async_copy(src.at[pl.ds(src_row, 1)], dst.at[pl.ds(dst_row, 1)], sem)


def _dispatch_body(dest_ref, xn_ref, buf_in_hbm, buf_hbm, sem, *, tb, t_total):
    del buf_in_hbm
    base = pl.program_id(0) * tb

    def issue(t, c):
        for k in range(2):
            _row_copy(xn_ref, t, buf_hbm, dest_ref[k * t_total + base + t], sem).start(priority=k)
        return c

    lax.fori_loop(0, tb, issue, 0, unroll=8)

    def drain(t, c):
        for k in range(2):
            _row_copy(xn_ref, 0, buf_hbm, 0, sem).wait()
        return c

    lax.fori_loop(0, tb, drain, 0, unroll=8)


def moe_dispatch(dest_flat, xn, n_rows, tb):
    t, d = xn.shape
    buf0 = jnp.zeros((n_rows, d), xn.dtype)
    grid_spec = pltpu.PrefetchScalarGridSpec(
        num_scalar_prefetch=1,
        grid=(t // tb,),
        in_specs=[pl.BlockSpec((tb, d), lambda i, dest: (i, 0)), pl.BlockSpec(memory_space=pl.ANY)],
        out_specs=pl.BlockSpec(memory_space=pl.ANY),
        scratch_shapes=[pltpu.SemaphoreType.DMA(())],
    )
    return pl.pallas_call(
        functools.partial(_dispatch_body, tb=tb, t_total=t),
        grid_spec=grid_spec,
        out_shape=jax.ShapeDtypeStruct((n_rows, d), xn.dtype),
        input_output_aliases={2: 0},
        compiler_params=_params(("arbitrary",)),
        name="moe_dispatch",
    )(dest_flat, xn, buf0)


def _swiglu_hidden(xb, wg_ref, wu_ref, h_ref, widx):
    f = h_ref.shape[1]
    for c0 in range(0, f, MXU_TILE):
        c1 = min(c0 + MXU_TILE, f)
        gate = _dot(xb, wg_ref[widx + (slice(None), slice(c0, c1))])
        up = _dot(xb, wu_ref[widx + (slice(None), slice(c0, c1))])
        h_ref[:, c0:c1] = (gate * jax.nn.sigmoid(gate) * up).astype(BF16)


def _experts_body(be_ref, nv_ref, x_ref, wg_ref, wu_ref, wd_ref, o_ref, h_ref):
    del be_ref
    valid = pl.program_id(0) < nv_ref[0]

    @pl.when(valid)
    def _():
        _swiglu_hidden(x_ref[...].astype(BF16), wg_ref, wu_ref, h_ref, (0,))
        o_ref[...] = _dot(h_ref[...], wd_ref[0])

    @pl.when(jnp.logical_not(valid))
    def _():
        o_ref[...] = jnp.zeros_like(o_ref)


def moe_experts(block_e, n_valid, buf, wg, wu, wd, layer, tm):
    n_rows, d = buf.shape
    f = wg.shape[3]
    once = pl.Buffered(1)
    grid_spec = pltpu.PrefetchScalarGridSpec(
        num_scalar_prefetch=2,
        grid=(n_rows // tm,),
        in_specs=[pl.BlockSpec((tm, d), lambda i, be, nv: (i, 0)),
                  pl.BlockSpec((None, 1, d, f), lambda i, be, nv: (layer, be[i], 0, 0), pipeline_mode=once),
                  pl.BlockSpec((None, 1, d, f), lambda i, be, nv: (layer, be[i], 0, 0), pipeline_mode=once),
                  pl.BlockSpec((None, 1, f, d), lambda i, be, nv: (layer, be[i], 0, 0), pipeline_mode=once)],
        out_specs=pl.BlockSpec((tm, d), lambda i, be, nv: (i, 0)),
        scratch_shapes=[pltpu.VMEM((tm, f), BF16)],
    )
    return pl.pallas_call(
        _experts_body,
        grid_spec=grid_spec,
        out_shape=jax.ShapeDtypeStruct((n_rows, d), F32),
        compiler_params=_params(("arbitrary",), vmem=VMEM_LIMIT_BIG),
        name="moe_experts",
    )(block_e, n_valid, buf, wg, wu, wd)


def _combine_body(dest_ref, x_ref, gate_ref, eo_hbm, o_ref, buf_ref, sems, *, tb, t_total):
    i = pl.program_id(0)
    slot = i % 2

    def start_block(block, to_slot):
        base = block * tb

        def issue(t, c):
            for k in range(2):
                _row_copy(eo_hbm, dest_ref[k * t_total + base + t], buf_ref.at[to_slot, k], t,
                          sems.at[to_slot]).start(priority=k)
            return c

        lax.fori_loop(0, tb, issue, 0, unroll=8)

    @pl.when(i == 0)
    def _():
        start_block(0, 0)

    @pl.when(i + 1 < pl.num_programs(0))
    def _():
        start_block(i + 1, 1 - slot)

    def drain(t, c):
        for k in range(2):
            _row_copy(eo_hbm, 0, buf_ref.at[slot, k], 0, sems.at[slot]).wait()
        return c

    lax.fori_loop(0, tb, drain, 0, unroll=8)
    gates = gate_ref[...]
    o_ref[...] = x_ref[...] + gates[:, 0:1] * buf_ref[slot, 0] + gates[:, 1:2] * buf_ref[slot, 1]


def moe_combine(dest_flat, x, gates_t, expert_out, tb):
    t, d = x.shape
    grid_spec = pltpu.PrefetchScalarGridSpec(
        num_scalar_prefetch=1,
        grid=(t // tb,),
        in_specs=[pl.BlockSpec((tb, d), lambda i, dest: (i, 0)),
                  pl.BlockSpec((tb, 2), lambda i, dest: (i, 0)),
                  pl.BlockSpec(memory_space=pl.ANY)],
        out_specs=pl.BlockSpec((tb, d), lambda i, dest: (i, 0)),
        scratch_shapes=[pltpu.VMEM((2, 2, tb, d), F32), pltpu.SemaphoreType.DMA((2,))],
    )
    return pl.pallas_call(
        functools.partial(_combine_body, tb=tb, t_total=t),
        grid_spec=grid_spec,
        out_shape=jax.ShapeDtypeStruct((t, d), F32),
        compiler_params=_params(("arbitrary",)),
        name="moe_combine",
    )(dest_flat, x, gates_t, expert_out)


def moe_layer(att, wo, res, g, w_router, b_router, wg, wu, wd, layer, tiles):
    t, d = res.shape
    tm = tiles["moe_tm"]
    x, xn, idx, gates, rank, cnt = moe_router(att, wo, res, g, w_router, b_router, tiles["router_tm"])
    counts = cnt[:, 0]
    padded = ((counts + tm - 1) // tm) * tm
    pends = jnp.cumsum(padded)
    pstarts = pends - padded
    group_start = sum(jnp.where(idx == e, pstarts[e], 0) for e in range(N_EXPERTS))
    dest_flat = (group_start + rank).reshape(-1)
    n_rows = 2 * t + N_EXPERTS * tm
    n_blocks = n_rows // tm
    block_row = jnp.arange(n_blocks, dtype=I32) * tm
    block_e = jnp.minimum(sum((block_row >= pends[e]).astype(I32) for e in range(N_EXPERTS)),
                          N_EXPERTS - 1)
    n_valid = (pends[-1:] // tm).astype(I32)
    buf = moe_dispatch(dest_flat, xn, n_rows, tiles["moe_tb"])
    eo = moe_experts(block_e, n_valid, buf, wg, wu, wd, layer, tm)
    return moe_combine(dest_flat, x, gates.T, eo, tiles["moe_tb"])


def _tiles(t, s):
    def fit(n, want):
        while n % want:
            want //= 2
        return want
    return dict(
        proj_tm=fit(t, 1024), attn_tq=fit(s, 256), mla_tq=fit(s, 512), prep_tm=fit(s, 512),
        ffn_tm=fit(t, 512), router_tm=fit(t, 512), moe_tm=fit(t, 512), moe_tb=fit(t, 512),
    )


def _mla_score_bound(g_q_nope, g_q_pe, g_k_nope, g_k_pe, scale):
    q2 = NOPE_DIM * jnp.max(g_q_nope * g_q_nope) + ROPE_DIM * jnp.max(g_q_pe * g_q_pe)
    k2 = NOPE_DIM * jnp.max(g_k_nope * g_k_nope) + ROPE_DIM * jnp.max(g_k_pe * g_k_pe)
    return (scale * jnp.sqrt(q2 * k2) * 1.02 + 1e-2).reshape(1)


def _rope_tables(seq):
    pos = jnp.arange(seq, dtype=F32)
    inv_freq = ROPE_THETA ** (-jnp.arange(0, ROPE_DIM, 2, dtype=F32) / ROPE_DIM)
    ang = pos[:, None] * inv_freq[None, :]
    ang = jnp.concatenate([ang, ang], axis=-1)
    reps = LANES // ROPE_DIM
    return jnp.tile(jnp.cos(ang), (1, reps)), jnp.tile(jnp.sin(ang), (1, reps))


def kernel(x, g_mix, g_ffn, sb_w_qkv, sb_w_o, kv_g_src, kv_w_a, kv_g_a, kv_w_b, kv_g_k_nope, kv_g_k_pe, mla_w_q_a, mla_g_q_a, mla_w_q_b, mla_g_q_nope, mla_g_q_pe, mla_w_o, ffn_w_gate, ffn_w_up, ffn_w_down, moe_w_router, moe_b_router, moe_w_gate, moe_w_up, moe_w_down):
    b, s, d = x.shape
    t = b * s
    depth = g_mix.shape[0]
    n_a = sb_w_qkv.shape[0]
    tiles = _tiles(t, s)
    cos_t, sin_t = _rope_tables(s)
    h = x.reshape(t, d)
    hw = N_HEADS * SB_HEAD_DIM
    kcat = vcat = None
    ffn_w = [w.astype(BF16) for w in (ffn_w_gate, ffn_w_up, ffn_w_down)]
    moe_w = [w.astype(BF16) for w in (moe_w_gate, moe_w_up, moe_w_down)]

    for i in range(depth):
        if i < n_a:
            col_scale = jnp.concatenate([jnp.full((hw,), LOG2E / math.sqrt(SB_HEAD_DIM), F32),
                                         jnp.ones((2 * hw,), F32)])
            w_qkv = (sb_w_qkv[i] * col_scale).astype(BF16)
            qkv = rms_matmul(h, g_mix[i], w_qkv, BF16, tiles["ffn_tm"], w_qkv.shape[1])
            att = sb_attention(qkv.reshape(b, s, 3 * hw), tiles["attn_tq"])
            w_o = sb_w_o[i].astype(BF16)
        else:
            j = i - n_a
            if kcat is None:
                pad = jnp.zeros((d, LANES - ROPE_DIM), F32)
                w_a = jnp.concatenate([kv_w_a, pad], axis=1).astype(BF16)
                kva = rms_matmul(h, kv_g_src, w_a, F32, tiles["proj_tm"], w_a.shape[1])
                w_b = kv_w_b.reshape(KV_RANK, N_HEADS, NOPE_DIM + V_DIM)
                wn = w_b[:, :, :NOPE_DIM].reshape(KV_RANK, N_HEADS * NOPE_DIM).astype(BF16)
                wv = w_b[:, :, NOPE_DIM:].reshape(KV_RANK, N_HEADS * V_DIM).astype(BF16)
                g_pe_pad = jnp.concatenate([kv_g_k_pe, jnp.zeros((LANES - ROPE_DIM,), F32)]).reshape(1, LANES)
                kcat, vcat = kv_prep(kva, kv_g_a.reshape(1, KV_RANK), g_pe_pad, cos_t, sin_t, wn, wv,
                                     jnp.tile(kv_g_k_nope, N_HEADS).reshape(1, -1), tiles["prep_tm"], s)
                kcat = kcat.reshape(b, s, -1)
                vcat = vcat.reshape(b, s, -1)
            qa = rms_matmul(h, g_mix[i], mla_w_q_a[j].astype(BF16), F32, tiles["proj_tm"], mla_w_q_a.shape[2])
            w_qb = mla_w_q_b[j].reshape(-1, N_HEADS, NOPE_DIM + ROPE_DIM)
            wqn = w_qb[:, :, :NOPE_DIM].reshape(-1, N_HEADS * NOPE_DIM).astype(BF16)
            wqp = w_qb[:, :, NOPE_DIM:].reshape(-1, N_HEADS * ROPE_DIM).astype(BF16)
            scale = LOG2E / math.sqrt(NOPE_DIM + ROPE_DIM)
            shift = _mla_score_bound(mla_g_q_nope[j], mla_g_q_pe[j], kv_g_k_nope, kv_g_k_pe, scale)
            qcat = q_prep(qa, mla_g_q_a[j].reshape(1, -1), wqn, wqp,
                          jnp.tile(mla_g_q_nope[j], N_HEADS).reshape(1, -1),
                          jnp.tile(mla_g_q_pe[j], N_HEADS).reshape(1, -1),
                          cos_t, sin_t, shift, tiles["prep_tm"], s, scale)
            att = mla_attention(shift, qcat.reshape(b, s, -1), kcat, vcat, tiles["mla_tq"])
            w_o = mla_w_o[j].astype(BF16)

        att = att.reshape(t, -1)
        m = i // 2
        if i % 2 == 0:
            h = dense_ffn(att, w_o, h, g_ffn[i], *ffn_w, m, tiles["ffn_tm"])
        else:
            h = moe_layer(att, w_o, h, g_ffn[i], moe_w_router[m], moe_b_router[m], *moe_w, m, tiles)
    return h.reshape(b, s, d)
```
